```python
import jax, jax.numpy as jnp
from jax import lax
import numpy as np

D_MODEL = 1024
BATCH = 4
SEQ = 8192
DEPTH = 1

CTX_LEN = 256
GRID_W = 64
D_CONV = 1024
CONV_W = 3
N_Q_HEADS = 8
N_KV_HEADS = 2
HEAD_DIM = 128
GROUP = N_Q_HEADS // N_KV_HEADS
ROPE_THETA = 10000.0
D_FF = 2816
Q_BLOCK = 128
EPS = 1e-6
N_MOD = 9

Q_W = N_Q_HEADS * HEAD_DIM
KV_W = N_KV_HEADS * HEAD_DIM
PIECE_WIDTHS = (D_CONV, D_CONV, D_CONV, Q_W, KV_W, KV_W, 2 * D_MODEL)
D_IN = int(sum(PIECE_WIDTHS))
SPLITS = tuple(int(s) for s in np.cumsum(PIECE_WIDTHS)[:-1])

kernel_name = "hybrid_shortconv_gqa_macaron_dit_layer"


def rms_norm(x, g):
    xf = x.astype(jnp.float32)
    y = xf * lax.rsqrt(jnp.mean(xf * xf, axis=-1, keepdims=True) + EPS)
    return (y * g.astype(jnp.float32)).astype(x.dtype)


def modulate(h, shift, scale):
    return h * (1.0 + scale) + shift


def swiglu(h, w_in, w_out):
    a, b = jnp.split(h @ w_in, 2, axis=-1)
    return (jax.nn.silu(a) * b) @ w_out


def axial_rope_tables(n_tokens):
    rows = n_tokens // GRID_W
    row = jnp.repeat(jnp.arange(rows), GRID_W).astype(jnp.float32)
    col = jnp.tile(jnp.arange(GRID_W), rows).astype(jnp.float32)
    n_freq = HEAD_DIM // 4
    inv = ROPE_THETA ** (-jnp.arange(n_freq, dtype=jnp.float32) / n_freq)
    ang = jnp.stack([row[:, None] * inv, col[:, None] * inv], axis=1)
    return jnp.cos(ang), jnp.sin(ang)


def apply_axial_rope(x, cos, sin):
    xs = x.reshape(x.shape[:3] + (2, 2, HEAD_DIM // 4))
    x1, x2 = xs[..., 0, :], xs[..., 1, :]
    c = cos[None, :, None].astype(x.dtype)
    s = sin[None, :, None].astype(x.dtype)
    out = jnp.stack([x1 * c - x2 * s, x2 * c + x1 * s], axis=-2)
    return out.reshape(x.shape)


def short_conv(u, w):
    L = u.shape[1]
    pad = CONV_W // 2
    up = jnp.pad(u, ((0, 0), (pad, pad), (0, 0)))
    y = up[:, 0:L] * w[0]
    for j in range(1, CONV_W):
        y = y + up[:, j:j + L] * w[j]
    return y


def conv_branch(bg, cg, vc, conv_w):
    return bg * short_conv(cg * vc, conv_w)


def gqa_attend(q, k, v):
    s = jnp.einsum('btkgd,blkd->bkgtl', q, k, preferred_element_type=jnp.float32) * (HEAD_DIM ** -0.5)
    p = jax.nn.softmax(s, axis=-1).astype(v.dtype)
    return jnp.einsum('bkgtl,blkd->btkgd', p, v)


def latent_attention(q_lat, k_all, v_all):
    B, S = q_lat.shape[:2]
    nblk = S // Q_BLOCK
    qb = q_lat.reshape(B, nblk, Q_BLOCK, N_KV_HEADS, GROUP, HEAD_DIM).swapaxes(0, 1)
    ob = lax.map(lambda qi: gqa_attend(qi, k_all, v_all), qb)
    return ob.swapaxes(0, 1).reshape(B, S, Q_W)


def split_heads(q, k, v, q_g, k_g):
    B, L = q.shape[:2]
    q = rms_norm(q.reshape(B, L, N_Q_HEADS, HEAD_DIM), q_g)
    k = rms_norm(k.reshape(B, L, N_KV_HEADS, HEAD_DIM), k_g)
    v = v.reshape(B, L, N_KV_HEADS, HEAD_DIM)
    return q, k, v


def merge_branches(y_conv, o_attn, gate_logits, w_bc, w_ba, w_out):
    g_conv, g_attn = jnp.split(jax.nn.sigmoid(gate_logits), 2, axis=-1)
    return (g_conv * (y_conv @ w_bc) + g_attn * (o_attn @ w_ba)) @ w_out


def setup_inputs(seed: int = 0) -> dict:
    key = jax.random.key(seed)
    ks = jax.random.split(key, 24)

    def nrm(k, shape, scale):
        return jax.random.normal(k, shape, jnp.float32) * scale

    def gain(k, shape):
        return 1.0 + 0.1 * jax.random.normal(k, shape, jnp.float32)

    L = DEPTH
    return {
        "x": nrm(ks[0], (BATCH, SEQ, D_MODEL), 1.0),
        "c": nrm(ks[1], (BATCH, D_MODEL), 1.0),
        "ctx": nrm(ks[2], (BATCH, CTX_LEN, D_MODEL), 1.0),
        "c_ctx": nrm(ks[3], (D_MODEL,), 1.0),
        "w_mod": nrm(ks[4], (L, D_MODEL, N_MOD * D_MODEL), 0.5 * D_MODEL ** -0.5),
        "b_mod": nrm(ks[5], (L, N_MOD * D_MODEL), 0.02),
        "norm1_g": gain(ks[6], (L, D_MODEL)),
        "norm2_g": gain(ks[7], (L, D_MODEL)),
        "norm3_g": gain(ks[8], (L, D_MODEL)),
        "ffn1_w_in": nrm(ks[9], (L, D_MODEL, 2 * D_FF), D_MODEL ** -0.5),
        "ffn1_w_out": nrm(ks[10], (L, D_FF, D_MODEL), D_FF ** -0.5),
        "w_in": nrm(ks[11], (L, D_MODEL, D_IN), D_MODEL ** -0.5),
        "conv_w": nrm(ks[12], (L, CONV_W, D_CONV), CONV_W ** -0.5),
        "q_norm_g": gain(ks[13], (L, HEAD_DIM)),
        "k_norm_g": gain(ks[14], (L, HEAD_DIM)),
        "w_branch_conv": nrm(ks[15], (L, D_CONV, D_MODEL), D_CONV ** -0.5),
        "w_branch_attn": nrm(ks[16], (L, Q_W, D_MODEL), Q_W ** -0.5),
        "w_out": nrm(ks[17], (L, D_MODEL, D_MODEL), D_MODEL ** -0.5),
        "ffn2_w_in": nrm(ks[18], (L, D_MODEL, 2 * D_FF), D_MODEL ** -0.5),
        "ffn2_w_out": nrm(ks[19], (L, D_FF, D_MODEL), D_FF ** -0.5),
        "final_g": gain(ks[20], (D_MODEL,)),
    }


def reference(x, c, ctx, c_ctx, w_mod, b_mod, norm1_g, norm2_g, norm3_g,
              ffn1_w_in, ffn1_w_out, w_in, conv_w, q_norm_g, k_norm_g,
              w_branch_conv, w_branch_attn, w_out, ffn2_w_in, ffn2_w_out, final_g):
    B, S, _ = x.shape
    cos, sin = axial_rope_tables(S)
    cx = ctx
    for layer in range(DEPTH):
        last = layer == DEPTH - 1
        mod_lat = (jax.nn.silu(c) @ w_mod[layer] + b_mod[layer])[:, None, :]
        mod_ctx = (jax.nn.silu(c_ctx) @ w_mod[layer] + b_mod[layer])[None, None, :]
        ml = jnp.split(mod_lat, N_MOD, axis=-1)
        mc = jnp.split(mod_ctx, N_MOD, axis=-1)

        x = x + 0.5 * ml[2] * swiglu(modulate(rms_norm(x, norm1_g[layer]), ml[0], ml[1]),
                                      ffn1_w_in[layer], ffn1_w_out[layer])
        cx = cx + 0.5 * mc[2] * swiglu(modulate(rms_norm(cx, norm1_g[layer]), mc[0], mc[1]),
                                        ffn1_w_in[layer], ffn1_w_out[layer])

        hx = modulate(rms_norm(x, norm2_g[layer]), ml[3], ml[4])
        hc = modulate(rms_norm(cx, norm2_g[layer]), mc[3], mc[4])
        bg_l, cg_l, vc_l, q_l, k_l, v_l, gt_l = jnp.split(hx @ w_in[layer], SPLITS, axis=-1)
        bg_c, cg_c, vc_c, q_c, k_c, v_c, gt_c = jnp.split(hc @ w_in[layer], SPLITS, axis=-1)

        y_conv_l = conv_branch(bg_l, cg_l, vc_l, conv_w[layer])

        q_l, k_l, v_l = split_heads(q_l, k_l, v_l, q_norm_g[layer], k_norm_g[layer])
        q_c, k_c, v_c = split_heads(q_c, k_c, v_c, q_norm_g[layer], k_norm_g[layer])
        q_l = apply_axial_rope(q_l, cos, sin)
        k_l = apply_axial_rope(k_l, cos, sin)
        k_all = jnp.concatenate([k_c, k_l], axis=1)
        v_all = jnp.concatenate([v_c, v_l], axis=1)
        o_l = latent_attention(q_l, k_all, v_all)

        x = x + ml[5] * merge_branches(y_conv_l, o_l, gt_l, w_branch_conv[layer],
                                       w_branch_attn[layer], w_out[layer])

        if not last:
            y_conv_c = conv_branch(bg_c, cg_c, vc_c, conv_w[layer])
            o_c = gqa_attend(q_c.reshape(B, cx.shape[1], N_KV_HEADS, GROUP, HEAD_DIM), k_c, v_c)
            o_c = o_c.reshape(B, cx.shape[1], Q_W)
            cx = cx + mc[5] * merge_branches(y_conv_c, o_c, gt_c, w_branch_conv[layer],
                                             w_branch_attn[layer], w_out[layer])
            cx = cx + 0.5 * mc[8] * swiglu(modulate(rms_norm(cx, norm3_g[layer]), mc[6], mc[7]),
                                            ffn2_w_in[layer], ffn2_w_out[layer])

        x = x + 0.5 * ml[8] * swiglu(modulate(rms_norm(x, norm3_g[layer]), ml[6], ml[7]),
                                      ffn2_w_in[layer], ffn2_w_out[layer])
    return rms_norm(x, final_g)
```

```python
import functools
import math

import jax
import jax.numpy as jnp
from jax import lax
from jax.experimental import pallas as pl
from jax.experimental.pallas import tpu as pltpu

F32 = jnp.float32
BF16 = jnp.bfloat16

HEAD_DIM = 128
N_Q_HEADS = 8
N_KV_HEADS = 2
GROUP = N_Q_HEADS // N_KV_HEADS
GRID_W = 64
ROPE_THETA = 10000.0
EPS = 1e-6
N_MOD = 9
CONV_W = 3

V7X_VMEM_BYTES = 64 * 1024 * 1024
VMEM_LIMIT = V7X_VMEM_BYTES - 8 * 1024 * 1024
SUBLANES = 8

LOG2E = 1.4426950408889634


def _const_spec(shape):
    nd = len(shape)
    return pl.BlockSpec(shape, lambda *_: (0,) * nd, pipeline_mode=pl.Buffered(1))


def _rms(x, g):
    return x * lax.rsqrt(jnp.mean(x * x, axis=-1, keepdims=True) + EPS) * g


def _silu(a):
    return a * jax.nn.sigmoid(a)


def _dot(a, b):
    return jnp.dot(a, b, preferred_element_type=F32)


def _mod_kernel(c_ref, w_ref, b_ref, o_ref):
    s = _silu(c_ref[...])
    w = w_ref[...]
    s_hi = s.astype(BF16)
    s_lo = (s - s_hi.astype(F32)).astype(BF16)
    w_hi = w.astype(BF16)
    w_lo = (w - w_hi.astype(F32)).astype(BF16)
    o_ref[...] = _dot(s_hi, w_hi) + (_dot(s_lo, w_hi) + _dot(s_hi, w_lo)) + b_ref[...]


def _mod_call(cvec, w_mod, b_mod):
    rows, d = cvec.shape
    n = w_mod.shape[1]
    tn = d
    return pl.pallas_call(
        _mod_kernel,
        grid=(n // tn,),
        in_specs=[
            pl.BlockSpec((rows, d), lambda j: (0, 0)),
            pl.BlockSpec((d, tn), lambda j: (0, j)),
            pl.BlockSpec((1, tn), lambda j: (0, j)),
        ],
        out_specs=pl.BlockSpec((rows, tn), lambda j: (0, j)),
        out_shape=jax.ShapeDtypeStruct((rows, n), F32),
        name="mod",
    )(cvec, w_mod, b_mod.reshape(1, n))


def _ffn_body(x, shift, scale, gate, g, win_ref, wout_ref, d_ff):
    h = _rms(x, g) * (1.0 + scale) + shift
    hb = h.astype(BF16)
    a = _dot(hb, win_ref[:, :d_ff])
    b = _dot(hb, win_ref[:, d_ff:])
    act = (_silu(a) * b).astype(BF16)
    return x + (0.5 * gate) * _dot(act, wout_ref[...])


def _ffn_kernel(x_ref, mod_ref, g_ref, win_ref, wout_ref, o_ref, *, mod_base, d_ff):
    x = x_ref[0]
    m = mod_ref[0]
    o_ref[0] = _ffn_body(x, m[mod_base:mod_base + 1], m[mod_base + 1:mod_base + 2],
                         m[mod_base + 2:mod_base + 3], g_ref[...], win_ref, wout_ref, d_ff)


def _ffn_call(x, mod, mod_row, g, w_in, w_out, *, mod_base, tm, name):
    bsz, seq, d = x.shape
    d_ff = w_out.shape[0]
    kern = functools.partial(_ffn_kernel, mod_base=mod_base, d_ff=d_ff)
    return pl.pallas_call(
        kern,
        grid=(bsz, seq // tm),
        in_specs=[
            pl.BlockSpec((1, tm, d), lambda b, i: (b, i, 0)),
            pl.BlockSpec((1, N_MOD, d), lambda b, i: (mod_row(b), 0, 0)),
            _const_spec((1, d)),
            _const_spec(w_in.shape),
            _const_spec(w_out.shape),
        ],
        out_specs=pl.BlockSpec((1, tm, d), lambda b, i: (b, i, 0)),
        out_shape=jax.ShapeDtypeStruct(x.shape, F32),
        compiler_params=pltpu.CompilerParams(
            dimension_semantics=("parallel", "parallel"), vmem_limit_bytes=VMEM_LIMIT),
        name=name,
    )(x, mod, g.reshape(1, d), w_in, w_out)


def _rope_rot(x):
    n = x.shape[-1]
    lane = lax.broadcasted_iota(jnp.int32, x.shape, x.ndim - 1)
    quarter = HEAD_DIM // 4
    fwd = pltpu.roll(x, n - quarter, x.ndim - 1)
    bwd = pltpu.roll(x, quarter, x.ndim - 1)
    return jnp.where((lane % (2 * quarter)) < quarter, fwd, bwd)


def _head_norm(xh, g):
    return xh * lax.rsqrt(jnp.mean(xh * xh, axis=-1, keepdims=True) + EPS) * g


def _proj_kernel(x_ref, mod_ref, g_ref, w_ref, cos_ref, sin_ref, qg_ref, kg_ref,
                 u_ref, bg_ref, gt_ref, q_ref, k_ref, v_ref, *, splits):
    x = x_ref[0]
    m = mod_ref[0]
    h = _rms(x, g_ref[...]) * (1.0 + m[4:5]) + m[3:4]
    hb = h.astype(BF16)
    s_b, s_c, s_v, s_q, s_k, s_vv = splits
    n_in = w_ref.shape[1]

    bg_ref[0] = _dot(hb, w_ref[:, :s_b])
    u_ref[0] = _dot(hb, w_ref[:, s_b:s_c]) * _dot(hb, w_ref[:, s_c:s_v])
    gt_ref[0] = jax.nn.sigmoid(_dot(hb, w_ref[:, s_vv:n_in]))

    cos = cos_ref[...]
    sin = sin_ref[...]
    q = _dot(hb, w_ref[:, s_v:s_q])
    for hd in range(N_Q_HEADS):
        sl = slice(hd * HEAD_DIM, (hd + 1) * HEAD_DIM)
        qh = _head_norm(q[:, sl], qg_ref[...])
        q_ref[0, :, sl] = (qh * cos + _rope_rot(qh) * sin).astype(BF16)
    k = _dot(hb, w_ref[:, s_q:s_k])
    for hd in range(N_KV_HEADS):
        sl = slice(hd * HEAD_DIM, (hd + 1) * HEAD_DIM)
        kh = _head_norm(k[:, sl], kg_ref[...])
        k_ref[0, :, sl] = (kh * cos + _rope_rot(kh) * sin).astype(BF16)
    v_ref[0] = _dot(hb, w_ref[:, s_k:s_vv]).astype(BF16)


def _proj_call(x, mod, g, w_in, cos_t, sin_t, q_g, k_g, *, splits, tm):
    bsz, seq, d = x.shape
    d_conv = splits[0]
    q_w = splits[3] - splits[2]
    kv_w = splits[4] - splits[3]
    gates_w = w_in.shape[1] - splits[5]
    kern = functools.partial(_proj_kernel, splits=splits)
    tok = lambda w: pl.BlockSpec((1, tm, w), lambda b, i: (b, i, 0))
    return pl.pallas_call(
        kern,
        grid=(bsz, seq // tm),
        in_specs=[
            tok(d),
            pl.BlockSpec((1, N_MOD, d), lambda b, i: (b, 0, 0)),
            _const_spec((1, d)),
            _const_spec(w_in.shape),
            pl.BlockSpec((tm, HEAD_DIM), lambda b, i: (i, 0)),
            pl.BlockSpec((tm, HEAD_DIM), lambda b, i: (i, 0)),
            _const_spec((1, HEAD_DIM)),
            _const_spec((1, HEAD_DIM)),
        ],
        out_specs=[tok(d_conv), tok(d_conv), tok(gates_w), tok(q_w), tok(kv_w), tok(kv_w)],
        out_shape=[
            jax.ShapeDtypeStruct((bsz, seq, d_conv), F32),
            jax.ShapeDtypeStruct((bsz, seq, d_conv), F32),
            jax.ShapeDtypeStruct((bsz, seq, gates_w), F32),
            jax.ShapeDtypeStruct((bsz, seq, q_w), BF16),
            jax.ShapeDtypeStruct((bsz, seq, kv_w), BF16),
            jax.ShapeDtypeStruct((bsz, seq, kv_w), BF16),
        ],
        compiler_params=pltpu.CompilerParams(
            dimension_semantics=("parallel", "parallel"), vmem_limit_bytes=VMEM_LIMIT),
        name="proj_latent",
    )(x, mod, g.reshape(1, d), w_in, cos_t, sin_t, q_g.reshape(1, HEAD_DIM), k_g.reshape(1, HEAD_DIM))


def _proj_ctx_kernel(x_ref, mod_ref, g_ref, w_ref, kg_ref, k_ref, v_ref, *, kv_w):
    x = x_ref[0]
    m = mod_ref[0]
    h = _rms(x, g_ref[...]) * (1.0 + m[4:5]) + m[3:4]
    hb = h.astype(BF16)
    k = _dot(hb, w_ref[:, :kv_w])
    for hd in range(N_KV_HEADS):
        sl = slice(hd * HEAD_DIM, (hd + 1) * HEAD_DIM)
        k_ref[0, :, sl] = _head_norm(k[:, sl], kg_ref[...]).astype(BF16)
    v_ref[0] = _dot(hb, w_ref[:, kv_w:]).astype(BF16)


def _proj_ctx_call(cx, mod, mod_row, g, w_in, k_g, *, splits):
    bsz, seq, d = cx.shape
    kv_w = splits[4] - splits[3]
    col_blk = splits[3] // (2 * kv_w)
    assert col_blk * 2 * kv_w == splits[3]
    kern = functools.partial(_proj_ctx_kernel, kv_w=kv_w)
    return pl.pallas_call(
        kern,
        grid=(bsz,),
        in_specs=[
            pl.BlockSpec((1, seq, d), lambda b: (b, 0, 0)),
            pl.BlockSpec((1, N_MOD, d), lambda b: (mod_row, 0, 0)),
            _const_spec((1, d)),
            pl.BlockSpec((d, 2 * kv_w), lambda b: (0, col_blk)),
            _const_spec((1, HEAD_DIM)),
        ],
        out_specs=[pl.BlockSpec((1, seq, kv_w), lambda b: (b, 0, 0))] * 2,
        out_shape=[jax.ShapeDtypeStruct((bsz, seq, kv_w), BF16)] * 2,
        compiler_params=pltpu.CompilerParams(dimension_semantics=("parallel",)),
        name="proj_context",
    )(cx, mod, g.reshape(1, d), w_in, k_g.reshape(1, HEAD_DIM))


def _attn_kernel(q_ref, kc_ref, vc_ref, kl_ref, vl_ref, o_ref, *, tq, ck, n_chunks):
    q = jnp.concatenate(
        [q_ref[0, :, g * HEAD_DIM:(g + 1) * HEAD_DIM] for g in range(GROUP)], axis=0)
    c = (HEAD_DIM ** -0.5) * LOG2E

    def step(k, v, carry):
        m, l, acc = carry
        s = lax.dot_general(q, k, (((1,), (1,)), ((), ())), preferred_element_type=F32) * c
        m_new = jnp.maximum(m, jnp.max(s, axis=-1, keepdims=True))
        alpha = jnp.exp2(m - m_new)
        p = jnp.exp2(s - m_new)
        l = alpha * l + jnp.sum(p, axis=-1, keepdims=True)
        acc = alpha * acc + _dot(p.astype(BF16), v)
        return m_new, l, acc

    rows = GROUP * tq
    carry = (jnp.full((rows, 1), -jnp.inf, F32), jnp.zeros((rows, 1), F32),
             jnp.zeros((rows, HEAD_DIM), F32))
    carry = step(kc_ref[0], vc_ref[0], carry)

    def body(j, carry):
        off = pl.multiple_of(j * ck, ck)
        return step(kl_ref[0, pl.ds(off, ck), :], vl_ref[0, pl.ds(off, ck), :], carry)

    _, l, acc = lax.fori_loop(0, n_chunks, body, carry)
    o = acc / l
    for g in range(GROUP):
        o_ref[0, :, g * HEAD_DIM:(g + 1) * HEAD_DIM] = o[g * tq:(g + 1) * tq].astype(BF16)


def _attn_call(q, k_c, v_c, k_l, v_l, *, tq, ck):
    bsz, seq, q_w = q.shape
    ctx_len = k_c.shape[1]
    gw = GROUP * HEAD_DIM
    kern = functools.partial(_attn_kernel, tq=tq, ck=ck, n_chunks=seq // ck)
    return pl.pallas_call(
        kern,
        grid=(bsz, N_KV_HEADS, seq // tq),
        in_specs=[
            pl.BlockSpec((1, tq, gw), lambda b, h, i: (b, i, h)),
            pl.BlockSpec((1, ctx_len, HEAD_DIM), lambda b, h, i: (b, 0, h)),
            pl.BlockSpec((1, ctx_len, HEAD_DIM), lambda b, h, i: (b, 0, h)),
            pl.BlockSpec((1, seq, HEAD_DIM), lambda b, h, i: (b, 0, h)),
            pl.BlockSpec((1, seq, HEAD_DIM), lambda b, h, i: (b, 0, h)),
        ],
        out_specs=pl.BlockSpec((1, tq, gw), lambda b, h, i: (b, i, h)),
        out_shape=jax.ShapeDtypeStruct((bsz, seq, q_w), BF16),
        compiler_params=pltpu.CompilerParams(
            dimension_semantics=("parallel", "parallel", "parallel"), vmem_limit_bytes=VMEM_LIMIT),
        name="attention",
    )(q, k_c, v_c, k_l, v_l)


def _merge_kernel(x_ref, u_ref, up_ref, un_ref, bg_ref, gt_ref, o_ref, mod_ref, cw_ref,
                  wbc_ref, wba_ref, wout_ref, g3_ref, win2_ref, wout2_ref, fg_ref, out_ref,
                  *, tm, d_ff):
    i = pl.program_id(1)
    last = pl.num_programs(1) - 1
    x = x_ref[0]
    m = mod_ref[0]
    u = u_ref[0]
    d = x.shape[-1]

    row = lax.broadcasted_iota(jnp.int32, (tm, 1), 0)
    prev_edge = jnp.where(i > 0, up_ref[0, SUBLANES - 1:SUBLANES, :], 0.0)
    next_edge = jnp.where(i < last, un_ref[0, 0:1, :], 0.0)
    u_prev = jnp.where(row == 0, prev_edge, pltpu.roll(u, 1, 0))
    u_next = jnp.where(row == tm - 1, next_edge, pltpu.roll(u, tm - 1, 0))
    cw = cw_ref[...]
    y_conv = bg_ref[0] * (u_prev * cw[0:1] + u * cw[1:2] + u_next * cw[2:3])

    gt = gt_ref[0]
    merged = (gt[:, :d] * _dot(y_conv.astype(BF16), wbc_ref[...])
              + gt[:, d:] * _dot(o_ref[0], wba_ref[...]))
    x = x + m[5:6] * _dot(merged.astype(BF16), wout_ref[...])

    x = _ffn_body(x, m[6:7], m[7:8], m[8:9], g3_ref[...], win2_ref, wout2_ref, d_ff)
    out_ref[0] = _rms(x, fg_ref[...])


def _merge_call(x, u, bg, gt, o, mod, conv_w, w_bc, w_ba, w_out, g3, w_in2, w_out2, fg, *, tm):
    bsz, seq, d = x.shape
    d_ff = w_out2.shape[0]
    hb = tm // SUBLANES
    n_halo = seq // SUBLANES
    kern = functools.partial(_merge_kernel, tm=tm, d_ff=d_ff)
    tok = lambda w: pl.BlockSpec((1, tm, w), lambda b, i: (b, i, 0))
    return pl.pallas_call(
        kern,
        grid=(bsz, seq // tm),
        in_specs=[
            tok(d),
            tok(u.shape[-1]),
            pl.BlockSpec((1, SUBLANES, u.shape[-1]), lambda b, i: (b, jnp.maximum(i * hb - 1, 0), 0)),
            pl.BlockSpec((1, SUBLANES, u.shape[-1]),
                         lambda b, i: (b, jnp.minimum((i + 1) * hb, n_halo - 1), 0)),
            tok(bg.shape[-1]),
            tok(gt.shape[-1]),
            tok(o.shape[-1]),
            pl.BlockSpec((1, N_MOD, d), lambda b, i: (b, 0, 0)),
            _const_spec(conv_w.shape),
            _const_spec(w_bc.shape),
            _const_spec(w_ba.shape),
            _const_spec(w_out.shape),
            _const_spec((1, d)),
            _const_spec(w_in2.shape),
            _const_spec(w_out2.shape),
            _const_spec((1, d)),
        ],
        out_specs=tok(d),
        out_shape=jax.ShapeDtypeStruct(x.shape, F32),
        compiler_params=pltpu.CompilerParams(
            dimension_semantics=("parallel", "parallel"), vmem_limit_bytes=VMEM_LIMIT),
        name="merge_ffn2",
    )(x, u, u, u, bg, gt, o, mod, conv_w, w_bc, w_ba, w_out, g3.reshape(1, d), w_in2, w_out2,
      fg.reshape(1, d))


def _rope_tables(seq):
    rows = seq // GRID_W
    row = jnp.repeat(jnp.arange(rows), GRID_W).astype(F32)
    col = jnp.tile(jnp.arange(GRID_W), rows).astype(F32)
    n_freq = HEAD_DIM // 4
    inv = ROPE_THETA ** (-jnp.arange(n_freq, dtype=F32) / n_freq)
    ang_r = row[:, None] * inv
    ang_c = col[:, None] * inv
    cos_t = jnp.concatenate([jnp.cos(ang_r)] * 2 + [jnp.cos(ang_c)] * 2, axis=-1)
    sin_t = jnp.concatenate([-jnp.sin(ang_r), jnp.sin(ang_r), -jnp.sin(ang_c), jnp.sin(ang_c)], axis=-1)
    return cos_t, sin_t


def kernel(x, c, ctx, c_ctx, w_mod, b_mod, norm1_g, norm2_g, norm3_g, ffn1_w_in, ffn1_w_out,
           w_in, conv_w, q_norm_g, k_norm_g, w_branch_conv, w_branch_attn, w_out, ffn2_w_in,
           ffn2_w_out, final_g):
    bsz, seq, d = x.shape
    depth = w_mod.shape[0]
    assert depth == 1, "single-layer problem"
    d_conv = conv_w.shape[-1]
    q_w = N_Q_HEADS * HEAD_DIM
    kv_w = N_KV_HEADS * HEAD_DIM
    widths = (d_conv, d_conv, d_conv, q_w, kv_w, kv_w)
    splits = tuple(int(sum(widths[:j + 1])) for j in range(len(widths)))
    assert w_in.shape[-1] == splits[-1] + 2 * d

    ctx_row = bsz
    pad_rows = -(bsz + 1) % SUBLANES
    cvec = jnp.concatenate([c, c_ctx[None, :], jnp.zeros((pad_rows, d), F32)], axis=0)
    mod = _mod_call(cvec, w_mod[0], b_mod[0]).reshape(cvec.shape[0], N_MOD, d)

    w1_in = ffn1_w_in[0].astype(BF16)
    w1_out = ffn1_w_out[0].astype(BF16)
    w2_in = ffn2_w_in[0].astype(BF16)
    w2_out = ffn2_w_out[0].astype(BF16)
    w_in_b = w_in[0].astype(BF16)
    w_bc = w_branch_conv[0].astype(BF16)
    w_ba = w_branch_attn[0].astype(BF16)
    w_o = w_out[0].astype(BF16)

    tm = 512
    x1 = _ffn_call(x, mod, lambda b: b, norm1_g[0], w1_in, w1_out, mod_base=0, tm=tm, name="ffn1_latent")
    cx1 = _ffn_call(ctx, mod, lambda b: ctx_row, norm1_g[0], w1_in, w1_out, mod_base=0,
                    tm=ctx.shape[1], name="ffn1_context")

    cos_t, sin_t = _rope_tables(seq)
    u, bg, gt, q, k_l, v_l = _proj_call(x1, mod, norm2_g[0], w_in_b, cos_t, sin_t, q_norm_g[0],
                                        k_norm_g[0], splits=splits, tm=tm)
    k_c, v_c = _proj_ctx_call(cx1, mod, ctx_row, norm2_g[0], w_in_b, k_norm_g[0], splits=splits)

    o = _attn_call(q, k_c, v_c, k_l, v_l, tq=512, ck=512)

    return _merge_call(x1, u, bg, gt, o, mod, conv_w[0], w_bc, w_ba, w_o, norm3_g[0], w2_in, w2_out,
                       final_g, tm=256)
```

```python
import functools
import math

import jax
import jax.numpy as jnp
from jax import lax
from jax.experimental import pallas as pl
from jax.experimental.pallas import tpu as pltpu

F32 = jnp.float32
BF16 = jnp.bfloat16

HEAD_DIM = 128
N_Q_HEADS = 8
N_KV_HEADS = 2
GROUP = N_Q_HEADS // N_KV_HEADS
GRID_W = 64
ROPE_THETA = 10000.0
EPS = 1e-6
N_MOD = 9
CONV_W = 3

V7X_VMEM_BYTES = 64 * 1024 * 1024
VMEM_LIMIT = V7X_VMEM_BYTES - 8 * 1024 * 1024
SUBLANES = 8
BF16_SUBLANES = 16

LOG2E = 1.4426950408889634


def _const_spec(shape):
    nd = len(shape)
    return pl.BlockSpec(shape, lambda *_: (0,) * nd, pipeline_mode=pl.Buffered(1))


def _rms(x, g):
    return x * lax.rsqrt(jnp.mean(x * x, axis=-1, keepdims=True) + EPS) * g


def _silu(a):
    return a * jax.nn.sigmoid(a)


def _dot(a, b):
    return jnp.dot(a, b, preferred_element_type=F32)


def _mod_kernel(c_ref, w_ref, b_ref, o_ref):
    s = _silu(c_ref[...])
    w = w_ref[...]
    s_hi = s.astype(BF16)
    s_lo = (s - s_hi.astype(F32)).astype(BF16)
    w_hi = w.astype(BF16)
    w_lo = (w - w_hi.astype(F32)).astype(BF16)
    o_ref[...] = _dot(s_hi, w_hi) + (_dot(s_lo, w_hi) + _dot(s_hi, w_lo)) + b_ref[...]


def _mod_call(cvec, w_mod, b_mod):
    rows, d = cvec.shape
    n = w_mod.shape[1]
    tn = d
    return pl.pallas_call(
        _mod_kernel,
        grid=(n // tn,),
        in_specs=[
            pl.BlockSpec((rows, d), lambda j: (0, 0)),
            pl.BlockSpec((d, tn), lambda j: (0, j)),
            pl.BlockSpec((1, tn), lambda j: (0, j)),
        ],
        out_specs=pl.BlockSpec((rows, tn), lambda j: (0, j)),
        out_shape=jax.ShapeDtypeStruct((rows, n), F32),
        name="mod",
    )(cvec, w_mod, b_mod.reshape(1, n))


def _ffn_body(x, shift, scale, gate, g, win_ref, wout_ref, d_ff):
    h = _rms(x, g) * (1.0 + scale) + shift
    hb = h.astype(BF16)
    a = _dot(hb, win_ref[:, :d_ff])
    b = _dot(hb, win_ref[:, d_ff:])
    act = (_silu(a) * b).astype(BF16)
    return x + (0.5 * gate) * _dot(act, wout_ref[...])


def _ffn_kernel(x_ref, mod_ref, g_ref, win_ref, wout_ref, o_ref, *, mod_base, d_ff):
    x = x_ref[0]
    m = mod_ref[0]
    o_ref[0] = _ffn_body(x, m[mod_base:mod_base + 1], m[mod_base + 1:mod_base + 2],
                         m[mod_base + 2:mod_base + 3], g_ref[...], win_ref, wout_ref, d_ff)


def _ffn_call(x, mod, mod_row, g, w_in, w_out, *, mod_base, tm, name):
    bsz, seq, d = x.shape
    d_ff = w_out.shape[0]
    kern = functools.partial(_ffn_kernel, mod_base=mod_base, d_ff=d_ff)
    return pl.pallas_call(
        kern,
        grid=(bsz, seq // tm),
        in_specs=[
            pl.BlockSpec((1, tm, d), lambda b, i: (b, i, 0)),
            pl.BlockSpec((1, N_MOD, d), lambda b, i: (mod_row(b), 0, 0)),
            _const_spec((1, d)),
            _const_spec(w_in.shape),
            _const_spec(w_out.shape),
        ],
        out_specs=pl.BlockSpec((1, tm, d), lambda b, i: (b, i, 0)),
        out_shape=jax.ShapeDtypeStruct(x.shape, F32),
        compiler_params=pltpu.CompilerParams(
            dimension_semantics=("parallel", "parallel"), vmem_limit_bytes=VMEM_LIMIT),
        name=name,
    )(x, mod, g.reshape(1, d), w_in, w_out)


def _rope_rot(x):
    n = x.shape[-1]
    lane = lax.broadcasted_iota(jnp.int32, x.shape, x.ndim - 1)
    quarter = HEAD_DIM // 4
    fwd = pltpu.roll(x, n - quarter, x.ndim - 1)
    bwd = pltpu.roll(x, quarter, x.ndim - 1)
    return jnp.where((lane % (2 * quarter)) < quarter, fwd, bwd)


def _head_norm(xh, g):
    return xh * lax.rsqrt(jnp.mean(xh * xh, axis=-1, keepdims=True) + EPS) * g


def _proj_kernel(x_ref, mod_ref, g_ref, w_ref, cos_ref, sin_ref, qg_ref, kg_ref,
                 u_ref, bg_ref, gt_ref, q_ref, k_ref, v_ref, *, splits):
    x = x_ref[0]
    m = mod_ref[0]
    h = _rms(x, g_ref[...]) * (1.0 + m[4:5]) + m[3:4]
    hb = h.astype(BF16)
    s_b, s_c, s_v, s_q, s_k, s_vv = splits
    n_in = w_ref.shape[1]

    bg_ref[0] = _dot(hb, w_ref[:, :s_b])
    u_ref[0] = _dot(hb, w_ref[:, s_b:s_c]) * _dot(hb, w_ref[:, s_c:s_v])
    gt_ref[0] = jax.nn.sigmoid(_dot(hb, w_ref[:, s_vv:n_in]))

    cos = cos_ref[...]
    sin = sin_ref[...]
    q = _dot(hb, w_ref[:, s_v:s_q])
    qg = qg_ref[...] * ((HEAD_DIM ** -0.5) * LOG2E)
    for hd in range(N_Q_HEADS):
        sl = slice(hd * HEAD_DIM, (hd + 1) * HEAD_DIM)
        qh = _head_norm(q[:, sl], qg)
        q_ref[0, :, sl] = (qh * cos + _rope_rot(qh) * sin).astype(BF16)
    k = _dot(hb, w_ref[:, s_q:s_k])
    for hd in range(N_KV_HEADS):
        sl = slice(hd * HEAD_DIM, (hd + 1) * HEAD_DIM)
        kh = _head_norm(k[:, sl], kg_ref[...])
        k_ref[0, :, sl] = (kh * cos + _rope_rot(kh) * sin).astype(BF16)
    v_ref[0] = _dot(hb, w_ref[:, s_k:s_vv]).astype(BF16)


def _proj_call(x, mod, g, w_in, cos_t, sin_t, q_g, k_g, *, splits, tm):
    bsz, seq, d = x.shape
    d_conv = splits[0]
    q_w = splits[3] - splits[2]
    kv_w = splits[4] - splits[3]
    gates_w = w_in.shape[1] - splits[5]
    kern = functools.partial(_proj_kernel, splits=splits)
    tok = lambda w: pl.BlockSpec((1, tm, w), lambda b, i: (b, i, 0))
    return pl.pallas_call(
        kern,
        grid=(bsz, seq // tm),
        in_specs=[
            tok(d),
            pl.BlockSpec((1, N_MOD, d), lambda b, i: (b, 0, 0)),
            _const_spec((1, d)),
            _const_spec(w_in.shape),
            pl.BlockSpec((tm, HEAD_DIM), lambda b, i: (i, 0)),
            pl.BlockSpec((tm, HEAD_DIM), lambda b, i: (i, 0)),
            _const_spec((1, HEAD_DIM)),
            _const_spec((1, HEAD_DIM)),
        ],
        out_specs=[tok(d_conv), tok(d_conv), tok(gates_w), tok(q_w), tok(kv_w), tok(kv_w)],
        out_shape=[
            jax.ShapeDtypeStruct((bsz, seq, d_conv), F32),
            jax.ShapeDtypeStruct((bsz, seq, d_conv), F32),
            jax.ShapeDtypeStruct((bsz, seq, gates_w), F32),
            jax.ShapeDtypeStruct((bsz, seq, q_w), BF16),
            jax.ShapeDtypeStruct((bsz, seq, kv_w), BF16),
            jax.ShapeDtypeStruct((bsz, seq, kv_w), BF16),
        ],
        compiler_params=pltpu.CompilerParams(
            dimension_semantics=("parallel", "parallel"), vmem_limit_bytes=VMEM_LIMIT),
        name="proj_latent",
    )(x, mod, g.reshape(1, d), w_in, cos_t, sin_t, q_g.reshape(1, HEAD_DIM), k_g.reshape(1, HEAD_DIM))


def _proj_ctx_kernel(x_ref, mod_ref, g_ref, w_ref, kg_ref, k_ref, v_ref, *, kv_w):
    x = x_ref[0]
    m = mod_ref[0]
    h = _rms(x, g_ref[...]) * (1.0 + m[4:5]) + m[3:4]
    hb = h.astype(BF16)
    k = _dot(hb, w_ref[:, :kv_w])
    for hd in range(N_KV_HEADS):
        sl = slice(hd * HEAD_DIM, (hd + 1) * HEAD_DIM)
        k_ref[0, :, sl] = _head_norm(k[:, sl], kg_ref[...]).astype(BF16)
    v_ref[0] = _dot(hb, w_ref[:, kv_w:]).astype(BF16)


def _proj_ctx_call(cx, mod, mod_row, g, w_in, k_g, *, splits):
    bsz, seq, d = cx.shape
    kv_w = splits[4] - splits[3]
    col_blk = splits[3] // (2 * kv_w)
    assert col_blk * 2 * kv_w == splits[3]
    kern = functools.partial(_proj_ctx_kernel, kv_w=kv_w)
    return pl.pallas_call(
        kern,
        grid=(bsz,),
        in_specs=[
            pl.BlockSpec((1, seq, d), lambda b: (b, 0, 0)),
            pl.BlockSpec((1, N_MOD, d), lambda b: (mod_row, 0, 0)),
            _const_spec((1, d)),
            pl.BlockSpec((d, 2 * kv_w), lambda b: (0, col_blk)),
            _const_spec((1, HEAD_DIM)),
        ],
        out_specs=[pl.BlockSpec((1, seq, kv_w), lambda b: (b, 0, 0))] * 2,
        out_shape=[jax.ShapeDtypeStruct((bsz, seq, kv_w), BF16)] * 2,
        compiler_params=pltpu.CompilerParams(dimension_semantics=("parallel",)),
        name="proj_context",
    )(cx, mod, g.reshape(1, d), w_in, k_g.reshape(1, HEAD_DIM))


def _attn_kernel(qt_ref, k_ref, vt_ref, o_ref, m_ref, acc_ref, *, qb, ck, n_chunks, unroll, lag):
    tq = qt_ref.shape[-1]
    d = o_ref.shape[2]
    blocks = [(g, t) for g in range(GROUP) for t in range(tq // qb)]
    m_ref[...] = jnp.full(m_ref.shape, -jnp.inf, F32)
    acc_ref[...] = jnp.zeros(acc_ref.shape, F32)

    def scores(j, b):
        g, t = blocks[b]
        off = pl.multiple_of(j * ck, ck)
        return _dot(k_ref[0, pl.ds(off, ck), :], qt_ref[0, g, :, t * qb:(t + 1) * qb])

    def update(j, b, s):
        m_old = m_ref[b]
        m_new = jnp.maximum(m_old, jnp.max(s, axis=0, keepdims=True))
        p = jnp.exp2(s - m_new).astype(BF16)
        acc_ref[b] = jnp.exp2(m_old - m_new) * acc_ref[b] + _dot(vt_ref[0, 0, j], p)
        m_ref[b] = m_new

    def body(it, carry):
        items = [(it * unroll + u, b) for u in range(unroll) for b in range(len(blocks))]
        pending = {}
        for i in range(len(items) + lag):
            if i < len(items):
                pending[i] = scores(*items[i])
            if i >= lag:
                update(*items[i - lag], pending.pop(i - lag))
        return carry

    lax.fori_loop(0, n_chunks // unroll, body, 0)
    for b, (g, t) in enumerate(blocks):
        acc = acc_ref[b]
        o_ref[0, g, :, t * qb:(t + 1) * qb] = (acc[:d] / acc[d:d + 1]).astype(BF16)


def _attn_call(qt, k_all, vt_all, *, tq, qb, unroll, lag):
    bsz, _, d, seq = qt.shape
    n_keys = k_all.shape[1]
    n_chunks, d_ext, ck = vt_all.shape[2:]
    assert n_chunks * ck == n_keys and tq % qb == 0 and n_chunks % unroll == 0
    n_blocks = GROUP * (tq // qb)
    kern = functools.partial(_attn_kernel, qb=qb, ck=ck, n_chunks=n_chunks, unroll=unroll, lag=lag)
    return pl.pallas_call(
        kern,
        grid=(bsz, N_KV_HEADS, seq // tq),
        in_specs=[
            pl.BlockSpec((1, GROUP, d, tq), lambda b, h, i: (b, h, 0, i)),
            pl.BlockSpec((1, n_keys, d), lambda b, h, i: (b, 0, h)),
            pl.BlockSpec((1, 1, n_chunks, d_ext, ck), lambda b, h, i: (b, h, 0, 0, 0)),
        ],
        out_specs=pl.BlockSpec((1, GROUP, d, tq), lambda b, h, i: (b, h, 0, i)),
        out_shape=jax.ShapeDtypeStruct(qt.shape, BF16),
        scratch_shapes=[
            pltpu.VMEM((n_blocks, 1, qb), F32),
            pltpu.VMEM((n_blocks, d_ext, qb), F32),
        ],
        compiler_params=pltpu.CompilerParams(
            dimension_semantics=("parallel", "parallel", "arbitrary"), vmem_limit_bytes=VMEM_LIMIT),
        name="attention",
    )(qt, k_all, vt_all)


def _merge_kernel(x_ref, u_ref, up_ref, un_ref, bg_ref, gt_ref, o_ref, mod_ref, cw_ref,
                  wbc_ref, wba_ref, wout_ref, g3_ref, win2_ref, wout2_ref, fg_ref, out_ref,
                  *, tm, d_ff):
    i = pl.program_id(1)
    last = pl.num_programs(1) - 1
    x = x_ref[0]
    m = mod_ref[0]
    u = u_ref[0]
    d = x.shape[-1]

    row = lax.broadcasted_iota(jnp.int32, (tm, 1), 0)
    prev_edge = jnp.where(i > 0, up_ref[0, SUBLANES - 1:SUBLANES, :], 0.0)
    next_edge = jnp.where(i < last, un_ref[0, 0:1, :], 0.0)
    u_prev = jnp.where(row == 0, prev_edge, pltpu.roll(u, 1, 0))
    u_next = jnp.where(row == tm - 1, next_edge, pltpu.roll(u, tm - 1, 0))
    cw = cw_ref[...]
    y_conv = bg_ref[0] * (u_prev * cw[0:1] + u * cw[1:2] + u_next * cw[2:3])

    gt = gt_ref[0]
    merged = (gt[:, :d] * _dot(y_conv.astype(BF16), wbc_ref[...])
              + gt[:, d:] * _dot(o_ref[0], wba_ref[...]))
    x = x + m[5:6] * _dot(merged.astype(BF16), wout_ref[...])

    x = _ffn_body(x, m[6:7], m[7:8], m[8:9], g3_ref[...], win2_ref, wout2_ref, d_ff)
    out_ref[0] = _rms(x, fg_ref[...])


def _merge_call(x, u, bg, gt, o, mod, conv_w, w_bc, w_ba, w_out, g3, w_in2, w_out2, fg, *, tm):
    bsz, seq, d = x.shape
    d_ff = w_out2.shape[0]
    hb = tm // SUBLANES
    n_halo = seq // SUBLANES
    kern = functools.partial(_merge_kernel, tm=tm, d_ff=d_ff)
    tok = lambda w: pl.BlockSpec((1, tm, w), lambda b, i: (b, i, 0))
    return pl.pallas_call(
        kern,
        grid=(bsz, seq // tm),
        in_specs=[
            tok(d),
            tok(u.shape[-1]),
            pl.BlockSpec((1, SUBLANES, u.shape[-1]), lambda b, i: (b, jnp.maximum(i * hb - 1, 0), 0)),
            pl.BlockSpec((1, SUBLANES, u.shape[-1]),
                         lambda b, i: (b, jnp.minimum((i + 1) * hb, n_halo - 1), 0)),
            tok(bg.shape[-1]),
            tok(gt.shape[-1]),
            tok(o.shape[-1]),
            pl.BlockSpec((1, N_MOD, d), lambda b, i: (b, 0, 0)),
            _const_spec(conv_w.shape),
            _const_spec(w_bc.shape),
            _const_spec(w_ba.shape),
            _const_spec(w_out.shape),
            _const_spec((1, d)),
            _const_spec(w_in2.shape),
            _const_spec(w_out2.shape),
            _const_spec((1, d)),
        ],
        out_specs=tok(d),
        out_shape=jax.ShapeDtypeStruct(x.shape, F32),
        compiler_params=pltpu.CompilerParams(
            dimension_semantics=("parallel", "parallel"), vmem_limit_bytes=VMEM_LIMIT),
        name="merge_ffn2",
    )(x, u, u, u, bg, gt, o, mod, conv_w, w_bc, w_ba, w_out, g3.reshape(1, d), w_in2, w_out2,
      fg.reshape(1, d))


def _rope_tables(seq):
    rows = seq // GRID_W
    row = jnp.repeat(jnp.arange(rows), GRID_W).astype(F32)
    col = jnp.tile(jnp.arange(GRID_W), rows).astype(F32)
    n_freq = HEAD_DIM // 4
    inv = ROPE_THETA ** (-jnp.arange(n_freq, dtype=F32) / n_freq)
    ang_r = row[:, None] * inv
    ang_c = col[:, None] * inv
    cos_t = jnp.concatenate([jnp.cos(ang_r)] * 2 + [jnp.cos(ang_c)] * 2, axis=-1)
    sin_t = jnp.concatenate([-jnp.sin(ang_r), jnp.sin(ang_r), -jnp.sin(ang_c), jnp.sin(ang_c)], axis=-1)
    return cos_t, sin_t


def kernel(x, c, ctx, c_ctx, w_mod, b_mod, norm1_g, norm2_g, norm3_g, ffn1_w_in, ffn1_w_out,
           w_in, conv_w, q_norm_g, k_norm_g, w_branch_conv, w_branch_attn, w_out, ffn2_w_in,
           ffn2_w_out, final_g):
    bsz, seq, d = x.shape
    depth = w_mod.shape[0]
    assert depth == 1, "single-layer problem"
    d_conv = conv_w.shape[-1]
    q_w = N_Q_HEADS * HEAD_DIM
    kv_w = N_KV_HEADS * HEAD_DIM
    widths = (d_conv, d_conv, d_conv, q_w, kv_w, kv_w)
    splits = tuple(int(sum(widths[:j + 1])) for j in range(len(widths)))
    assert w_in.shape[-1] == splits[-1] + 2 * d

    ctx_row = bsz
    pad_rows = -(bsz + 1) % SUBLANES
    cvec = jnp.concatenate([c, c_ctx[None, :], jnp.zeros((pad_rows, d), F32)], axis=0)
    mod = _mod_call(cvec, w_mod[0], b_mod[0]).reshape(cvec.shape[0], N_MOD, d)

    w1_in = ffn1_w_in[0].astype(BF16)
    w1_out = ffn1_w_out[0].astype(BF16)
    w2_in = ffn2_w_in[0].astype(BF16)
    w2_out = ffn2_w_out[0].astype(BF16)
    w_in_b = w_in[0].astype(BF16)
    w_bc = w_branch_conv[0].astype(BF16)
    w_ba = w_branch_attn[0].astype(BF16)
    w_o = w_out[0].astype(BF16)

    tm = 512
    x1 = _ffn_call(x, mod, lambda b: b, norm1_g[0], w1_in, w1_out, mod_base=0, tm=tm, name="ffn1_latent")
    cx1 = _ffn_call(ctx, mod, lambda b: ctx_row, norm1_g[0], w1_in, w1_out, mod_base=0,
                    tm=ctx.shape[1], name="ffn1_context")

    cos_t, sin_t = _rope_tables(seq)
    u, bg, gt, q, k_l, v_l = _proj_call(x1, mod, norm2_g[0], w_in_b, cos_t, sin_t, q_norm_g[0],
                                        k_norm_g[0], splits=splits, tm=tm)
    k_c, v_c = _proj_ctx_call(cx1, mod, ctx_row, norm2_g[0], w_in_b, k_norm_g[0], splits=splits)

    ck = 256
    n_keys = ctx.shape[1] + seq
    qt = q.reshape(bsz, seq, N_Q_HEADS, HEAD_DIM).transpose(0, 2, 3, 1)
    k_all = jnp.concatenate([k_c, k_l], axis=1)
    vt_all = jnp.concatenate([v_c, v_l], axis=1).reshape(
        bsz, n_keys // ck, ck, N_KV_HEADS, HEAD_DIM).transpose(0, 3, 1, 4, 2)
    ones_rows = jnp.ones(vt_all.shape[:3] + (BF16_SUBLANES, ck), BF16)
    vt_all = jnp.concatenate([vt_all, ones_rows], axis=3)
    ot = _attn_call(qt, k_all, vt_all, tq=512, qb=256, unroll=11, lag=5)
    o = ot.transpose(0, 3, 1, 2).reshape(bsz, seq, q_w)

    return _merge_call(x1, u, bg, gt, o, mod, conv_w[0], w_bc, w_ba, w_o, norm3_g[0], w2_in, w2_out,
                       final_g, tm=256)
```

```python
import functools
import math

import jax
import jax.numpy as jnp
from jax import lax
from jax.experimental import pallas as pl
from jax.experimental.pallas import tpu as pltpu

F32 = jnp.float32
BF16 = jnp.bfloat16

HEAD_DIM = 128
N_Q_HEADS = 8
N_KV_HEADS = 2
GROUP = N_Q_HEADS // N_KV_HEADS
GRID_W = 64
ROPE_THETA = 10000.0
EPS = 1e-6
N_MOD = 9
CONV_W = 3

V7X_VMEM_BYTES = 64 * 1024 * 1024
VMEM_LIMIT = V7X_VMEM_BYTES - 8 * 1024 * 1024
SUBLANES = 8
BF16_SUBLANES = 16

LOG2E = 1.4426950408889634


def _const_spec(shape):
    nd = len(shape)
    return pl.BlockSpec(shape, lambda *_: (0,) * nd, pipeline_mode=pl.Buffered(1))


def _rms(x, g):
    return x * lax.rsqrt(jnp.mean(x * x, axis=-1, keepdims=True) + EPS) * g


def _silu(a):
    return a * jax.nn.sigmoid(a)


def _dot(a, b):
    return jnp.dot(a, b, preferred_element_type=F32)


def _mod_kernel(c_ref, w_ref, b_ref, o_ref):
    s = _silu(c_ref[...])
    w = w_ref[...]
    s_hi = s.astype(BF16)
    s_lo = (s - s_hi.astype(F32)).astype(BF16)
    w_hi = w.astype(BF16)
    w_lo = (w - w_hi.astype(F32)).astype(BF16)
    o_ref[...] = _dot(s_hi, w_hi) + (_dot(s_lo, w_hi) + _dot(s_hi, w_lo)) + b_ref[...]


def _mod_call(cvec, w_mod, b_mod):
    rows, d = cvec.shape
    n = w_mod.shape[1]
    tn = d
    return pl.pallas_call(
        _mod_kernel,
        grid=(n // tn,),
        in_specs=[
            pl.BlockSpec((rows, d), lambda j: (0, 0)),
            pl.BlockSpec((d, tn), lambda j: (0, j)),
            pl.BlockSpec((1, tn), lambda j: (0, j)),
        ],
        out_specs=pl.BlockSpec((rows, tn), lambda j: (0, j)),
        out_shape=jax.ShapeDtypeStruct((rows, n), F32),
        name="mod",
    )(cvec, w_mod, b_mod.reshape(1, n))


def _ffn_body(x, shift, scale, gate, g, win_ref, wout_ref, d_ff):
    h = _rms(x, g) * (1.0 + scale) + shift
    hb = h.astype(BF16)
    a = _dot(hb, win_ref[:, :d_ff])
    b = _dot(hb, win_ref[:, d_ff:])
    act = (_silu(a) * b).astype(BF16)
    return x + (0.5 * gate) * _dot(act, wout_ref[...])


def _ffn_kernel(x_ref, mod_ref, g_ref, win_ref, wout_ref, o_ref, *, mod_base, d_ff):
    x = x_ref[0]
    m = mod_ref[0]
    o_ref[0] = _ffn_body(x, m[mod_base:mod_base + 1], m[mod_base + 1:mod_base + 2],
                         m[mod_base + 2:mod_base + 3], g_ref[...], win_ref, wout_ref, d_ff)


def _ffn_call(x, mod, mod_row, g, w_in, w_out, *, mod_base, tm, name):
    bsz, seq, d = x.shape
    d_ff = w_out.shape[0]
    kern = functools.partial(_ffn_kernel, mod_base=mod_base, d_ff=d_ff)
    return pl.pallas_call(
        kern,
        grid=(bsz, seq // tm),
        in_specs=[
            pl.BlockSpec((1, tm, d), lambda b, i: (b, i, 0)),
            pl.BlockSpec((1, N_MOD, d), lambda b, i: (mod_row(b), 0, 0)),
            _const_spec((1, d)),
            _const_spec(w_in.shape),
            _const_spec(w_out.shape),
        ],
        out_specs=pl.BlockSpec((1, tm, d), lambda b, i: (b, i, 0)),
        out_shape=jax.ShapeDtypeStruct(x.shape, F32),
        compiler_params=pltpu.CompilerParams(
            dimension_semantics=("parallel", "parallel"), vmem_limit_bytes=VMEM_LIMIT),
        name=name,
    )(x, mod, g.reshape(1, d), w_in, w_out)


def _rope_rot(x):
    n = x.shape[-1]
    lane = lax.broadcasted_iota(jnp.int32, x.shape, x.ndim - 1)
    quarter = HEAD_DIM // 4
    fwd = pltpu.roll(x, n - quarter, x.ndim - 1)
    bwd = pltpu.roll(x, quarter, x.ndim - 1)
    return jnp.where((lane % (2 * quarter)) < quarter, fwd, bwd)


def _head_norm(xh, g):
    return xh * lax.rsqrt(jnp.mean(xh * xh, axis=-1, keepdims=True) + EPS) * g


def _proj_kernel(x_ref, mod_ref, g_ref, w_ref, cos_ref, sin_ref, qg_ref, kg_ref,
                 u_ref, bg_ref, gt_ref, qt_ref, k_ref, vt_ref, *, splits):
    x = x_ref[0]
    m = mod_ref[0]
    h = _rms(x, g_ref[...]) * (1.0 + m[4:5]) + m[3:4]
    hb = h.astype(BF16)
    s_b, s_c, s_v, s_q, s_k, s_vv = splits
    n_in = w_ref.shape[1]

    cos = cos_ref[...]
    sin = sin_ref[...]
    q = _dot(hb, w_ref[:, s_v:s_q])
    k = _dot(hb, w_ref[:, s_q:s_k])
    qg = qg_ref[...] * ((HEAD_DIM ** -0.5) * LOG2E)
    for hd in range(N_Q_HEADS):
        qh = _head_norm(q[:, hd * HEAD_DIM:(hd + 1) * HEAD_DIM], qg)
        qt_ref[0, hd] = (qh * cos + _rope_rot(qh) * sin).T.astype(BF16)
    for hd in range(N_KV_HEADS):
        sl = slice(hd * HEAD_DIM, (hd + 1) * HEAD_DIM)
        kh = _head_norm(k[:, sl], kg_ref[...])
        k_ref[0, :, sl] = (kh * cos + _rope_rot(kh) * sin).astype(BF16)
    gt_ref[0] = jax.nn.sigmoid(_dot(hb, w_ref[:, s_vv:n_in]))
    v = _dot(hb, w_ref[:, s_k:s_vv])
    ck = vt_ref.shape[-1]
    for hd in range(N_KV_HEADS):
        for c in range(vt_ref.shape[2]):
            vt_ref[0, hd, c] = v[c * ck:(c + 1) * ck, hd * HEAD_DIM:(hd + 1) * HEAD_DIM].T.astype(BF16)
    u_ref[0] = _dot(hb, w_ref[:, s_b:s_c]) * _dot(hb, w_ref[:, s_c:s_v])
    bg_ref[0] = _dot(hb, w_ref[:, :s_b])


def _proj_call(x, mod, g, w_in, cos_t, sin_t, q_g, k_g, *, splits, tm, ck):
    bsz, seq, d = x.shape
    d_conv = splits[0]
    kv_w = splits[4] - splits[3]
    gates_w = w_in.shape[1] - splits[5]
    assert tm % ck == 0
    kern = functools.partial(_proj_kernel, splits=splits)
    tok = lambda w: pl.BlockSpec((1, tm, w), lambda b, i: (b, i, 0))
    return pl.pallas_call(
        kern,
        grid=(bsz, seq // tm),
        in_specs=[
            tok(d),
            pl.BlockSpec((1, N_MOD, d), lambda b, i: (b, 0, 0)),
            _const_spec((1, d)),
            _const_spec(w_in.shape),
            pl.BlockSpec((tm, HEAD_DIM), lambda b, i: (i, 0)),
            pl.BlockSpec((tm, HEAD_DIM), lambda b, i: (i, 0)),
            _const_spec((1, HEAD_DIM)),
            _const_spec((1, HEAD_DIM)),
        ],
        out_specs=[
            tok(d_conv), tok(d_conv), tok(gates_w),
            pl.BlockSpec((1, N_Q_HEADS, HEAD_DIM, tm), lambda b, i: (b, 0, 0, i)),
            tok(kv_w),
            pl.BlockSpec((1, N_KV_HEADS, tm // ck, HEAD_DIM, ck), lambda b, i: (b, 0, i, 0, 0)),
        ],
        out_shape=[
            jax.ShapeDtypeStruct((bsz, seq, d_conv), F32),
            jax.ShapeDtypeStruct((bsz, seq, d_conv), F32),
            jax.ShapeDtypeStruct((bsz, seq, gates_w), F32),
            jax.ShapeDtypeStruct((bsz, N_Q_HEADS, HEAD_DIM, seq), BF16),
            jax.ShapeDtypeStruct((bsz, seq, kv_w), BF16),
            jax.ShapeDtypeStruct((bsz, N_KV_HEADS, seq // ck, HEAD_DIM, ck), BF16),
        ],
        compiler_params=pltpu.CompilerParams(
            dimension_semantics=("parallel", "parallel"), vmem_limit_bytes=VMEM_LIMIT),
        name="proj_latent",
    )(x, mod, g.reshape(1, d), w_in, cos_t, sin_t, q_g.reshape(1, HEAD_DIM), k_g.reshape(1, HEAD_DIM))


def _proj_ctx_kernel(x_ref, mod_ref, g_ref, w_ref, kg_ref, k_ref, vt_ref, *, kv_w):
    x = x_ref[0]
    m = mod_ref[0]
    h = _rms(x, g_ref[...]) * (1.0 + m[4:5]) + m[3:4]
    hb = h.astype(BF16)
    k = _dot(hb, w_ref[:, :kv_w])
    v = _dot(hb, w_ref[:, kv_w:])
    for hd in range(N_KV_HEADS):
        sl = slice(hd * HEAD_DIM, (hd + 1) * HEAD_DIM)
        k_ref[0, :, sl] = _head_norm(k[:, sl], kg_ref[...]).astype(BF16)
        vt_ref[0, hd] = v[:, sl].T.astype(BF16)


def _proj_ctx_call(cx, mod, mod_row, g, w_in, k_g, *, splits):
    bsz, seq, d = cx.shape
    kv_w = splits[4] - splits[3]
    col_blk = splits[3] // (2 * kv_w)
    assert col_blk * 2 * kv_w == splits[3]
    kern = functools.partial(_proj_ctx_kernel, kv_w=kv_w)
    return pl.pallas_call(
        kern,
        grid=(bsz,),
        in_specs=[
            pl.BlockSpec((1, seq, d), lambda b: (b, 0, 0)),
            pl.BlockSpec((1, N_MOD, d), lambda b: (mod_row, 0, 0)),
            _const_spec((1, d)),
            pl.BlockSpec((d, 2 * kv_w), lambda b: (0, col_blk)),
            _const_spec((1, HEAD_DIM)),
        ],
        out_specs=[
            pl.BlockSpec((1, seq, kv_w), lambda b: (b, 0, 0)),
            pl.BlockSpec((1, N_KV_HEADS, HEAD_DIM, seq), lambda b: (b, 0, 0, 0)),
        ],
        out_shape=[
            jax.ShapeDtypeStruct((bsz, seq, kv_w), BF16),
            jax.ShapeDtypeStruct((bsz, N_KV_HEADS, HEAD_DIM, seq), BF16),
        ],
        compiler_params=pltpu.CompilerParams(dimension_semantics=("parallel",)),
        name="proj_context",
    )(cx, mod, g.reshape(1, d), w_in, k_g.reshape(1, HEAD_DIM))


def _attn_kernel(qt_ref, kc_ref, kl_ref, vtc_ref, vtl_ref, o_ref, m_ref, acc_ref, *, qb, unroll, lag):
    d, tq = qt_ref.shape[2:]
    ck = vtl_ref.shape[-1]
    n_lat = vtl_ref.shape[2]
    blocks = [(g, t) for g in range(GROUP) for t in range(tq // qb)]
    ones = jnp.ones((BF16_SUBLANES, ck), BF16)
    m_ref[...] = jnp.full(m_ref.shape, -jnp.inf, F32)
    acc_ref[...] = jnp.zeros(acc_ref.shape, F32)

    def scores(k, b):
        g, t = blocks[b]
        return _dot(k(), qt_ref[0, g, :, t * qb:(t + 1) * qb])

    def update(vt, b, s):
        m_old = m_ref[b]
        m_new = jnp.maximum(m_old, jnp.max(s, axis=0, keepdims=True))
        p = jnp.exp2(s - m_new).astype(BF16)
        pv = _dot(jnp.concatenate([vt(), ones], axis=0), p)
        acc_ref[b] = jnp.exp2(m_old - m_new) * acc_ref[b] + pv
        m_ref[b] = m_new

    def run(chunks):
        items = [(k, vt, b) for k, vt in chunks for b in range(len(blocks))]
        pending = {}
        for i in range(len(items) + lag):
            if i < len(items):
                k, _, b = items[i]
                pending[i] = scores(k, b)
            if i >= lag:
                _, vt, b = items[i - lag]
                update(vt, b, pending.pop(i - lag))

    def latent_chunk(j):
        off = pl.multiple_of(j * ck, ck)
        return (lambda: kl_ref[0, pl.ds(off, ck), :]), (lambda: vtl_ref[0, 0, j])

    run([((lambda: kc_ref[0]), (lambda: vtc_ref[0, 0]))])

    def body(it, carry):
        run([latent_chunk(it * unroll + u) for u in range(unroll)])
        return carry

    lax.fori_loop(0, n_lat // unroll, body, 0)
    for b, (g, t) in enumerate(blocks):
        acc = acc_ref[b]
        o = (acc[:d] / acc[d:d + 1]).T
        o_ref[0, t * qb:(t + 1) * qb, g * d:(g + 1) * d] = o.astype(BF16)


def _attn_call(qt, k_c, k_l, vt_c, vt_l, *, tq, qb, unroll, lag):
    bsz, n_heads, d, seq = qt.shape
    n_lat, _, ck = vt_l.shape[2:]
    assert k_c.shape[1] == ck and vt_c.shape[-1] == ck
    assert tq % qb == 0 and n_lat % unroll == 0
    n_blocks = GROUP * (tq // qb)
    kern = functools.partial(_attn_kernel, qb=qb, unroll=unroll, lag=lag)
    return pl.pallas_call(
        kern,
        grid=(bsz, N_KV_HEADS, seq // tq),
        in_specs=[
            pl.BlockSpec((1, GROUP, d, tq), lambda b, h, i: (b, h, 0, i)),
            pl.BlockSpec((1, ck, d), lambda b, h, i: (b, 0, h)),
            pl.BlockSpec((1, seq, d), lambda b, h, i: (b, 0, h)),
            pl.BlockSpec((1, 1, d, ck), lambda b, h, i: (b, h, 0, 0)),
            pl.BlockSpec((1, 1, n_lat, d, ck), lambda b, h, i: (b, h, 0, 0, 0)),
        ],
        out_specs=pl.BlockSpec((1, tq, GROUP * d), lambda b, h, i: (b, i, h)),
        out_shape=jax.ShapeDtypeStruct((bsz, seq, n_heads * d), BF16),
        scratch_shapes=[
            pltpu.VMEM((n_blocks, 1, qb), F32),
            pltpu.VMEM((n_blocks, d + BF16_SUBLANES, qb), F32),
        ],
        compiler_params=pltpu.CompilerParams(
            dimension_semantics=("parallel", "parallel", "arbitrary"), vmem_limit_bytes=VMEM_LIMIT),
        name="attention",
    )(qt, k_c, k_l, vt_c, vt_l)


def _merge_kernel(x_ref, u_ref, up_ref, un_ref, bg_ref, gt_ref, o_ref, mod_ref, cw_ref,
                  wbc_ref, wba_ref, wout_ref, g3_ref, win2_ref, wout2_ref, fg_ref, out_ref,
                  *, tm, d_ff):
    i = pl.program_id(1)
    last = pl.num_programs(1) - 1
    x = x_ref[0]
    m = mod_ref[0]
    u = u_ref[0]
    d = x.shape[-1]

    row = lax.broadcasted_iota(jnp.int32, (tm, 1), 0)
    prev_edge = jnp.where(i > 0, up_ref[0, SUBLANES - 1:SUBLANES, :], 0.0)
    next_edge = jnp.where(i < last, un_ref[0, 0:1, :], 0.0)
    u_prev = jnp.where(row == 0, prev_edge, pltpu.roll(u, 1, 0))
    u_next = jnp.where(row == tm - 1, next_edge, pltpu.roll(u, tm - 1, 0))
    cw = cw_ref[...]
    y_conv = bg_ref[0] * (u_prev * cw[0:1] + u * cw[1:2] + u_next * cw[2:3])

    gt = gt_ref[0]
    merged = (gt[:, :d] * _dot(y_conv.astype(BF16), wbc_ref[...])
              + gt[:, d:] * _dot(o_ref[0], wba_ref[...]))
    x = x + m[5:6] * _dot(merged.astype(BF16), wout_ref[...])

    x = _ffn_body(x, m[6:7], m[7:8], m[8:9], g3_ref[...], win2_ref, wout2_ref, d_ff)
    out_ref[0] = _rms(x, fg_ref[...])


def _merge_call(x, u, bg, gt, o, mod, conv_w, w_bc, w_ba, w_out, g3, w_in2, w_out2, fg, *, tm):
    bsz, seq, d = x.shape
    d_ff = w_out2.shape[0]
    hb = tm // SUBLANES
    n_halo = seq // SUBLANES
    kern = functools.partial(_merge_kernel, tm=tm, d_ff=d_ff)
    tok = lambda w: pl.BlockSpec((1, tm, w), lambda b, i: (b, i, 0))
    return pl.pallas_call(
        kern,
        grid=(bsz, seq // tm),
        in_specs=[
            tok(d),
            tok(u.shape[-1]),
            pl.BlockSpec((1, SUBLANES, u.shape[-1]), lambda b, i: (b, jnp.maximum(i * hb - 1, 0), 0)),
            pl.BlockSpec((1, SUBLANES, u.shape[-1]),
                         lambda b, i: (b, jnp.minimum((i + 1) * hb, n_halo - 1), 0)),
            tok(bg.shape[-1]),
            tok(gt.shape[-1]),
            tok(o.shape[-1]),
            pl.BlockSpec((1, N_MOD, d), lambda b, i: (b, 0, 0)),
            _const_spec(conv_w.shape),
            _const_spec(w_bc.shape),
            _const_spec(w_ba.shape),
            _const_spec(w_out.shape),
            _const_spec((1, d)),
            _const_spec(w_in2.shape),
            _const_spec(w_out2.shape),
            _const_spec((1, d)),
        ],
        out_specs=tok(d),
        out_shape=jax.ShapeDtypeStruct(x.shape, F32),
        compiler_params=pltpu.CompilerParams(
            dimension_semantics=("parallel", "parallel"), vmem_limit_bytes=VMEM_LIMIT),
        name="merge_ffn2",
    )(x, u, u, u, bg, gt, o, mod, conv_w, w_bc, w_ba, w_out, g3.reshape(1, d), w_in2, w_out2,
      fg.reshape(1, d))


def _rope_tables(seq):
    rows = seq // GRID_W
    row = jnp.repeat(jnp.arange(rows), GRID_W).astype(F32)
    col = jnp.tile(jnp.arange(GRID_W), rows).astype(F32)
    n_freq = HEAD_DIM // 4
    inv = ROPE_THETA ** (-jnp.arange(n_freq, dtype=F32) / n_freq)
    ang_r = row[:, None] * inv
    ang_c = col[:, None] * inv
    cos_t = jnp.concatenate([jnp.cos(ang_r)] * 2 + [jnp.cos(ang_c)] * 2, axis=-1)
    sin_t = jnp.concatenate([-jnp.sin(ang_r), jnp.sin(ang_r), -jnp.sin(ang_c), jnp.sin(ang_c)], axis=-1)
    return cos_t, sin_t


def kernel(x, c, ctx, c_ctx, w_mod, b_mod, norm1_g, norm2_g, norm3_g, ffn1_w_in, ffn1_w_out,
           w_in, conv_w, q_norm_g, k_norm_g, w_branch_conv, w_branch_attn, w_out, ffn2_w_in,
           ffn2_w_out, final_g):
    bsz, seq, d = x.shape
    depth = w_mod.shape[0]
    assert depth == 1, "single-layer problem"
    d_conv = conv_w.shape[-1]
    q_w = N_Q_HEADS * HEAD_DIM
    kv_w = N_KV_HEADS * HEAD_DIM
    widths = (d_conv, d_conv, d_conv, q_w, kv_w, kv_w)
    splits = tuple(int(sum(widths[:j + 1])) for j in range(len(widths)))
    assert w_in.shape[-1] == splits[-1] + 2 * d

    ctx_row = bsz
    pad_rows = -(bsz + 1) % SUBLANES
    cvec = jnp.concatenate([c, c_ctx[None, :], jnp.zeros((pad_rows, d), F32)], axis=0)
    mod = _mod_call(cvec, w_mod[0], b_mod[0]).reshape(cvec.shape[0], N_MOD, d)

    w1_in = ffn1_w_in[0].astype(BF16)
    w1_out = ffn1_w_out[0].astype(BF16)
    w2_in = ffn2_w_in[0].astype(BF16)
    w2_out = ffn2_w_out[0].astype(BF16)
    w_in_b = w_in[0].astype(BF16)
    w_bc = w_branch_conv[0].astype(BF16)
    w_ba = w_branch_attn[0].astype(BF16)
    w_o = w_out[0].astype(BF16)

    tm = 512
    x1 = _ffn_call(x, mod, lambda b: b, norm1_g[0], w1_in, w1_out, mod_base=0, tm=tm, name="ffn1_latent")
    cx1 = _ffn_call(ctx, mod, lambda b: ctx_row, norm1_g[0], w1_in, w1_out, mod_base=0,
                    tm=ctx.shape[1], name="ffn1_context")

    cos_t, sin_t = _rope_tables(seq)
    ck = ctx.shape[1]
    u, bg, gt, qt, k_l, vt_l = _proj_call(x1, mod, norm2_g[0], w_in_b, cos_t, sin_t, q_norm_g[0],
                                          k_norm_g[0], splits=splits, tm=tm, ck=ck)
    k_c, vt_c = _proj_ctx_call(cx1, mod, ctx_row, norm2_g[0], w_in_b, k_norm_g[0], splits=splits)
    o = _attn_call(qt, k_c, k_l, vt_c, vt_l, tq=512, qb=256, unroll=16, lag=5)

    return _merge_call(x1, u, bg, gt, o, mod, conv_w[0], w_bc, w_ba, w_o, norm3_g[0], w2_in, w2_out,
                       final_g, tm=256)
```

```python
import functools
import math

import jax
import jax.numpy as jnp
import numpy as np
from jax import lax
from jax.experimental import pallas as pl
from jax.experimental.pallas import tpu as pltpu

F32 = jnp.float32
BF16 = jnp.bfloat16

HEAD_DIM = 128
N_Q_HEADS = 8
N_KV_HEADS = 2
GROUP = N_Q_HEADS // N_KV_HEADS
GRID_W = 64
ROPE_THETA = 10000.0
EPS = 1e-6
N_MOD = 9
CONV_W = 3

V7X_VMEM_BYTES = 64 * 1024 * 1024
VMEM_LIMIT = V7X_VMEM_BYTES - 8 * 1024 * 1024
SUBLANES = 8
BF16_SUBLANES = 16

LOG2E = 1.4426950408889634


def _const_spec(shape):
    nd = len(shape)
    return pl.BlockSpec(shape, lambda *_: (0,) * nd, pipeline_mode=pl.Buffered(1))


def _rms(x, g):
    return x * lax.rsqrt(jnp.mean(x * x, axis=-1, keepdims=True) + EPS) * g


def _silu(a):
    return a * jax.nn.sigmoid(a)


def _dot(a, b):
    return jnp.dot(a, b, preferred_element_type=F32)


def _mod_kernel(c_ref, w_ref, b_ref, o_ref):
    s = _silu(c_ref[...])
    w = w_ref[...]
    s_hi = s.astype(BF16)
    s_lo = (s - s_hi.astype(F32)).astype(BF16)
    w_hi = w.astype(BF16)
    w_lo = (w - w_hi.astype(F32)).astype(BF16)
    o_ref[...] = _dot(s_hi, w_hi) + (_dot(s_lo, w_hi) + _dot(s_hi, w_lo)) + b_ref[...]


def _mod_call(cvec, w_mod, b_mod):
    rows, d = cvec.shape
    n = w_mod.shape[1]
    tn = d
    return pl.pallas_call(
        _mod_kernel,
        grid=(n // tn,),
        in_specs=[
            pl.BlockSpec((rows, d), lambda j: (0, 0)),
            pl.BlockSpec((d, tn), lambda j: (0, j)),
            pl.BlockSpec((1, tn), lambda j: (0, j)),
        ],
        out_specs=pl.BlockSpec((rows, tn), lambda j: (0, j)),
        out_shape=jax.ShapeDtypeStruct((rows, n), F32),
        name="mod",
    )(cvec, w_mod, b_mod.reshape(1, n))


def _row_split(tm, n_sub):
    assert tm % n_sub == 0
    return [slice(r * (tm // n_sub), (r + 1) * (tm // n_sub)) for r in range(n_sub)]


def _ffn_body(xs, shift, scale, gate, g, win_ref, wout_ref, d_ff):
    hbs = [(_rms(x, g) * (1.0 + scale) + shift).astype(BF16) for x in xs]
    acts = []
    for hb in hbs:
        a = _dot(hb, win_ref[:, :d_ff])
        b = _dot(hb, win_ref[:, d_ff:])
        acts.append((_silu(a) * b).astype(BF16))
    return [x + (0.5 * gate) * _dot(act, wout_ref[...]) for x, act in zip(xs, acts)]


def _ffn_kernel(x_ref, mod_ref, g_ref, win_ref, wout_ref, o_ref, *, mod_base, d_ff, n_sub):
    m = mod_ref[0]
    rows = _row_split(x_ref.shape[1], n_sub)
    outs = _ffn_body([x_ref[0, r, :] for r in rows], m[mod_base:mod_base + 1],
                     m[mod_base + 1:mod_base + 2], m[mod_base + 2:mod_base + 3], g_ref[...],
                     win_ref, wout_ref, d_ff)
    for r, out in zip(rows, outs):
        o_ref[0, r, :] = out


def _ffn_call(x, mod, mod_row, g, w_in, w_out, *, mod_base, tm, n_sub, name):
    bsz, seq, d = x.shape
    d_ff = w_out.shape[0]
    kern = functools.partial(_ffn_kernel, mod_base=mod_base, d_ff=d_ff, n_sub=n_sub)
    return pl.pallas_call(
        kern,
        grid=(bsz, seq // tm),
        in_specs=[
            pl.BlockSpec((1, tm, d), lambda b, i: (b, i, 0)),
            pl.BlockSpec((1, N_MOD, d), lambda b, i: (mod_row(b), 0, 0)),
            _const_spec((1, d)),
            _const_spec(w_in.shape),
            _const_spec(w_out.shape),
        ],
        out_specs=pl.BlockSpec((1, tm, d), lambda b, i: (b, i, 0)),
        out_shape=jax.ShapeDtypeStruct(x.shape, F32),
        compiler_params=pltpu.CompilerParams(
            dimension_semantics=("parallel", "parallel"), vmem_limit_bytes=VMEM_LIMIT),
        name=name,
    )(x, mod, g.reshape(1, d), w_in, w_out)


def _rope_rot(x):
    n = x.shape[-1]
    lane = lax.broadcasted_iota(jnp.int32, x.shape, x.ndim - 1)
    quarter = HEAD_DIM // 4
    fwd = pltpu.roll(x, n - quarter, x.ndim - 1)
    bwd = pltpu.roll(x, quarter, x.ndim - 1)
    return jnp.where((lane % (2 * quarter)) < quarter, fwd, bwd)


def _head_norm(xh, g):
    return xh * lax.rsqrt(jnp.mean(xh * xh, axis=-1, keepdims=True) + EPS) * g


def _proj_kernel(x_ref, mod_ref, g_ref, w_ref, cos_ref, sin_ref, qg_ref, kg_ref,
                 u_ref, bg_ref, gt_ref, qt_ref, k_ref, vt_ref, *, splits, n_sub):
    m = mod_ref[0]
    s_b, s_c, s_v, s_q, s_k, s_vv = splits
    n_in = w_ref.shape[1]
    ck = vt_ref.shape[-1]
    qg = qg_ref[...] * ((HEAD_DIM ** -0.5) * LOG2E)
    rows = _row_split(x_ref.shape[1], n_sub)
    hbs = [(_rms(x_ref[0, r, :], g_ref[...]) * (1.0 + m[4:5]) + m[3:4]).astype(BF16) for r in rows]

    for r, hb in zip(rows, hbs):
        cos = cos_ref[r, :]
        sin = sin_ref[r, :]
        q = _dot(hb, w_ref[:, s_v:s_q])
        k = _dot(hb, w_ref[:, s_q:s_k])
        for hd in range(N_Q_HEADS):
            qh = _head_norm(q[:, hd * HEAD_DIM:(hd + 1) * HEAD_DIM], qg)
            qt_ref[0, hd, :, r] = (qh * cos + _rope_rot(qh) * sin).T.astype(BF16)
        for hd in range(N_KV_HEADS):
            sl = slice(hd * HEAD_DIM, (hd + 1) * HEAD_DIM)
            kh = _head_norm(k[:, sl], kg_ref[...])
            k_ref[0, r, sl] = (kh * cos + _rope_rot(kh) * sin).astype(BF16)
        gt_ref[0, r, :] = jax.nn.sigmoid(_dot(hb, w_ref[:, s_vv:n_in]))
        v = _dot(hb, w_ref[:, s_k:s_vv])
        for hd in range(N_KV_HEADS):
            for c in range((r.stop - r.start) // ck):
                vc = v[c * ck:(c + 1) * ck, hd * HEAD_DIM:(hd + 1) * HEAD_DIM]
                vt_ref[0, hd, r.start // ck + c] = vc.T.astype(BF16)
        u_ref[0, r, :] = _dot(hb, w_ref[:, s_b:s_c]) * _dot(hb, w_ref[:, s_c:s_v])
        bg_ref[0, r, :] = _dot(hb, w_ref[:, :s_b])


def _proj_call(x, mod, g, w_in, cos_t, sin_t, q_g, k_g, *, splits, tm, ck, n_sub):
    bsz, seq, d = x.shape
    d_conv = splits[0]
    kv_w = splits[4] - splits[3]
    gates_w = w_in.shape[1] - splits[5]
    assert (tm // n_sub) % ck == 0
    kern = functools.partial(_proj_kernel, splits=splits, n_sub=n_sub)
    tok = lambda w: pl.BlockSpec((1, tm, w), lambda b, i: (b, i, 0))
    return pl.pallas_call(
        kern,
        grid=(bsz, seq // tm),
        in_specs=[
            tok(d),
            pl.BlockSpec((1, N_MOD, d), lambda b, i: (b, 0, 0)),
            _const_spec((1, d)),
            _const_spec(w_in.shape),
            pl.BlockSpec((tm, HEAD_DIM), lambda b, i: (i, 0)),
            pl.BlockSpec((tm, HEAD_DIM), lambda b, i: (i, 0)),
            _const_spec((1, HEAD_DIM)),
            _const_spec((1, HEAD_DIM)),
        ],
        out_specs=[
            tok(d_conv), tok(d_conv), tok(gates_w),
            pl.BlockSpec((1, N_Q_HEADS, HEAD_DIM, tm), lambda b, i: (b, 0, 0, i)),
            tok(kv_w),
            pl.BlockSpec((1, N_KV_HEADS, tm // ck, HEAD_DIM, ck), lambda b, i: (b, 0, i, 0, 0)),
        ],
        out_shape=[
            jax.ShapeDtypeStruct((bsz, seq, d_conv), F32),
            jax.ShapeDtypeStruct((bsz, seq, d_conv), F32),
            jax.ShapeDtypeStruct((bsz, seq, gates_w), F32),
            jax.ShapeDtypeStruct((bsz, N_Q_HEADS, HEAD_DIM, seq), BF16),
            jax.ShapeDtypeStruct((bsz, seq, kv_w), BF16),
            jax.ShapeDtypeStruct((bsz, N_KV_HEADS, seq // ck, HEAD_DIM, ck), BF16),
        ],
        compiler_params=pltpu.CompilerParams(
            dimension_semantics=("parallel", "parallel"), vmem_limit_bytes=VMEM_LIMIT),
        name="proj_latent",
    )(x, mod, g.reshape(1, d), w_in, cos_t, sin_t, q_g.reshape(1, HEAD_DIM), k_g.reshape(1, HEAD_DIM))


def _proj_ctx_kernel(x_ref, mod_ref, g_ref, w_ref, kg_ref, k_ref, vt_ref, *, kv_w):
    x = x_ref[0]
    m = mod_ref[0]
    h = _rms(x, g_ref[...]) * (1.0 + m[4:5]) + m[3:4]
    hb = h.astype(BF16)
    k = _dot(hb, w_ref[:, :kv_w])
    v = _dot(hb, w_ref[:, kv_w:])
    for hd in range(N_KV_HEADS):
        sl = slice(hd * HEAD_DIM, (hd + 1) * HEAD_DIM)
        k_ref[0, :, sl] = _head_norm(k[:, sl], kg_ref[...]).astype(BF16)
        vt_ref[0, hd] = v[:, sl].T.astype(BF16)


def _proj_ctx_call(cx, mod, mod_row, g, w_in, k_g, *, splits):
    bsz, seq, d = cx.shape
    kv_w = splits[4] - splits[3]
    col_blk = splits[3] // (2 * kv_w)
    assert col_blk * 2 * kv_w == splits[3]
    kern = functools.partial(_proj_ctx_kernel, kv_w=kv_w)
    return pl.pallas_call(
        kern,
        grid=(bsz,),
        in_specs=[
            pl.BlockSpec((1, seq, d), lambda b: (b, 0, 0)),
            pl.BlockSpec((1, N_MOD, d), lambda b: (mod_row, 0, 0)),
            _const_spec((1, d)),
            pl.BlockSpec((d, 2 * kv_w), lambda b: (0, col_blk)),
            _const_spec((1, HEAD_DIM)),
        ],
        out_specs=[
            pl.BlockSpec((1, seq, kv_w), lambda b: (b, 0, 0)),
            pl.BlockSpec((1, N_KV_HEADS, HEAD_DIM, seq), lambda b: (b, 0, 0, 0)),
        ],
        out_shape=[
            jax.ShapeDtypeStruct((bsz, seq, kv_w), BF16),
            jax.ShapeDtypeStruct((bsz, N_KV_HEADS, HEAD_DIM, seq), BF16),
        ],
        compiler_params=pltpu.CompilerParams(dimension_semantics=("parallel",)),
        name="proj_context",
    )(cx, mod, g.reshape(1, d), w_in, k_g.reshape(1, HEAD_DIM))


def _attn_kernel(qt_ref, kc_ref, kl_ref, vtc_ref, vtl_ref, o_ref, m_ref, acc_ref, *, qb, unroll, lag):
    d, tq = qt_ref.shape[2:]
    ck = vtl_ref.shape[-1]
    n_lat = vtl_ref.shape[2]
    blocks = [(g, t) for g in range(GROUP) for t in range(tq // qb)]
    ones = jnp.ones((BF16_SUBLANES, ck), BF16)
    m_ref[...] = jnp.full(m_ref.shape, -jnp.inf, F32)
    acc_ref[...] = jnp.zeros(acc_ref.shape, F32)

    def scores(k, b):
        g, t = blocks[b]
        return _dot(k(), qt_ref[0, g, :, t * qb:(t + 1) * qb])

    def update(vt, b, s):
        m_old = m_ref[b]
        m_new = jnp.maximum(m_old, jnp.max(s, axis=0, keepdims=True))
        p = jnp.exp2(s - m_new).astype(BF16)
        pv = _dot(jnp.concatenate([vt(), ones], axis=0), p)
        acc_ref[b] = jnp.exp2(m_old - m_new) * acc_ref[b] + pv
        m_ref[b] = m_new

    def run(chunks):
        items = [(k, vt, b) for k, vt in chunks for b in range(len(blocks))]
        pending = {}
        for i in range(len(items) + lag):
            if i < len(items):
                k, _, b = items[i]
                pending[i] = scores(k, b)
            if i >= lag:
                _, vt, b = items[i - lag]
                update(vt, b, pending.pop(i - lag))

    def latent_chunk(j):
        off = pl.multiple_of(j * ck, ck)
        return (lambda: kl_ref[0, pl.ds(off, ck), :]), (lambda: vtl_ref[0, 0, j])

    context_chunk = ((lambda: kc_ref[0]), (lambda: vtc_ref[0, 0]))
    if unroll == n_lat:
        run([context_chunk] + [latent_chunk(j) for j in range(n_lat)])
    else:
        run([context_chunk])

        def body(it, carry):
            run([latent_chunk(it * unroll + u) for u in range(unroll)])
            return carry

        lax.fori_loop(0, n_lat // unroll, body, 0)
    for b, (g, t) in enumerate(blocks):
        acc = acc_ref[b]
        o = (acc[:d] / acc[d:d + 1]).T
        o_ref[0, t * qb:(t + 1) * qb, g * d:(g + 1) * d] = o.astype(BF16)


def _attn_call(qt, k_c, k_l, vt_c, vt_l, *, tq, qb, unroll, lag):
    bsz, n_heads, d, seq = qt.shape
    n_lat, _, ck = vt_l.shape[2:]
    assert k_c.shape[1] == ck and vt_c.shape[-1] == ck
    assert tq % qb == 0 and n_lat % unroll == 0
    n_blocks = GROUP * (tq // qb)
    kern = functools.partial(_attn_kernel, qb=qb, unroll=unroll, lag=lag)
    return pl.pallas_call(
        kern,
        grid=(bsz, N_KV_HEADS, seq // tq),
        in_specs=[
            pl.BlockSpec((1, GROUP, d, tq), lambda b, h, i: (b, h, 0, i)),
            pl.BlockSpec((1, ck, d), lambda b, h, i: (b, 0, h)),
            pl.BlockSpec((1, seq, d), lambda b, h, i: (b, 0, h)),
            pl.BlockSpec((1, 1, d, ck), lambda b, h, i: (b, h, 0, 0)),
            pl.BlockSpec((1, 1, n_lat, d, ck), lambda b, h, i: (b, h, 0, 0, 0)),
        ],
        out_specs=pl.BlockSpec((1, tq, GROUP * d), lambda b, h, i: (b, i, h)),
        out_shape=jax.ShapeDtypeStruct((bsz, seq, n_heads * d), BF16),
        scratch_shapes=[
            pltpu.VMEM((n_blocks, 1, qb), F32),
            pltpu.VMEM((n_blocks, d + BF16_SUBLANES, qb), F32),
        ],
        compiler_params=pltpu.CompilerParams(
            dimension_semantics=("parallel", "parallel", "arbitrary"), vmem_limit_bytes=VMEM_LIMIT),
        name="attention",
    )(qt, k_c, k_l, vt_c, vt_l)


def _merge_kernel(x_ref, u_ref, up_ref, un_ref, bg_ref, gt_ref, o_ref, mod_ref, cw_ref,
                  wbc_ref, wba_ref, wout_ref, g3_ref, win2_ref, wout2_ref, fg_ref, out_ref,
                  *, tm, d_ff, n_sub):
    i = pl.program_id(1)
    last = pl.num_programs(1) - 1
    x = x_ref[0]
    m = mod_ref[0]
    u = u_ref[0]
    d = x.shape[-1]

    row = lax.broadcasted_iota(jnp.int32, (tm, 1), 0)
    prev_edge = jnp.where(i > 0, up_ref[0, SUBLANES - 1:SUBLANES, :], 0.0)
    next_edge = jnp.where(i < last, un_ref[0, 0:1, :], 0.0)
    u_prev = jnp.where(row == 0, prev_edge, pltpu.roll(u, 1, 0))
    u_next = jnp.where(row == tm - 1, next_edge, pltpu.roll(u, tm - 1, 0))
    cw = cw_ref[...]
    y_conv = bg_ref[0] * (u_prev * cw[0:1] + u * cw[1:2] + u_next * cw[2:3])

    rows = _row_split(tm, n_sub)
    yb = y_conv.astype(BF16)
    merged = [(gt_ref[0, r, :d] * _dot(yb[r], wbc_ref[...])
               + gt_ref[0, r, d:] * _dot(o_ref[0, r, :], wba_ref[...])).astype(BF16) for r in rows]
    xs = [x[r] + m[5:6] * _dot(mg, wout_ref[...]) for r, mg in zip(rows, merged)]
    xs = _ffn_body(xs, m[6:7], m[7:8], m[8:9], g3_ref[...], win2_ref, wout2_ref, d_ff)
    for r, xr in zip(rows, xs):
        out_ref[0, r, :] = _rms(xr, fg_ref[...])


def _merge_call(x, u, bg, gt, o, mod, conv_w, w_bc, w_ba, w_out, g3, w_in2, w_out2, fg, *, tm, n_sub):
    bsz, seq, d = x.shape
    d_ff = w_out2.shape[0]
    hb = tm // SUBLANES
    n_halo = seq // SUBLANES
    kern = functools.partial(_merge_kernel, tm=tm, d_ff=d_ff, n_sub=n_sub)
    tok = lambda w: pl.BlockSpec((1, tm, w), lambda b, i: (b, i, 0))
    return pl.pallas_call(
        kern,
        grid=(bsz, seq // tm),
        in_specs=[
            tok(d),
            tok(u.shape[-1]),
            pl.BlockSpec((1, SUBLANES, u.shape[-1]), lambda b, i: (b, jnp.maximum(i * hb - 1, 0), 0)),
            pl.BlockSpec((1, SUBLANES, u.shape[-1]),
                         lambda b, i: (b, jnp.minimum((i + 1) * hb, n_halo - 1), 0)),
            tok(bg.shape[-1]),
            tok(gt.shape[-1]),
            tok(o.shape[-1]),
            pl.BlockSpec((1, N_MOD, d), lambda b, i: (b, 0, 0)),
            _const_spec(conv_w.shape),
            _const_spec(w_bc.shape),
            _const_spec(w_ba.shape),
            _const_spec(w_out.shape),
            _const_spec((1, d)),
            _const_spec(w_in2.shape),
            _const_spec(w_out2.shape),
            _const_spec((1, d)),
        ],
        out_specs=tok(d),
        out_shape=jax.ShapeDtypeStruct(x.shape, F32),
        compiler_params=pltpu.CompilerParams(
            dimension_semantics=("parallel", "parallel"), vmem_limit_bytes=VMEM_LIMIT),
        name="merge_ffn2",
    )(x, u, u, u, bg, gt, o, mod, conv_w, w_bc, w_ba, w_out, g3.reshape(1, d), w_in2, w_out2,
      fg.reshape(1, d))


def _rope_tables(seq):
    f32 = np.float32
    pos = np.arange(seq)
    n_freq = HEAD_DIM // 4
    inv = np.power(f32(ROPE_THETA), -np.arange(n_freq, dtype=f32) / f32(n_freq)).astype(f32)
    ang_r = (pos // GRID_W).astype(f32)[:, None] * inv
    ang_c = (pos % GRID_W).astype(f32)[:, None] * inv
    cos_r, sin_r, cos_c, sin_c = (fn(a.astype(np.float64)).astype(f32)
                                  for a in (ang_r, ang_c) for fn in (np.cos, np.sin))
    cos_t = np.concatenate([cos_r, cos_r, cos_c, cos_c], axis=-1)
    sin_t = np.concatenate([-sin_r, sin_r, -sin_c, sin_c], axis=-1)
    return jnp.asarray(cos_t), jnp.asarray(sin_t)


def kernel(x, c, ctx, c_ctx, w_mod, b_mod, norm1_g, norm2_g, norm3_g, ffn1_w_in, ffn1_w_out,
           w_in, conv_w, q_norm_g, k_norm_g, w_branch_conv, w_branch_attn, w_out, ffn2_w_in,
           ffn2_w_out, final_g):
    bsz, seq, d = x.shape
    depth = w_mod.shape[0]
    assert depth == 1, "single-layer problem"
    d_conv = conv_w.shape[-1]
    q_w = N_Q_HEADS * HEAD_DIM
    kv_w = N_KV_HEADS * HEAD_DIM
    widths = (d_conv, d_conv, d_conv, q_w, kv_w, kv_w)
    splits = tuple(int(sum(widths[:j + 1])) for j in range(len(widths)))
    assert w_in.shape[-1] == splits[-1] + 2 * d

    ctx_row = bsz
    pad_rows = -(bsz + 1) % SUBLANES
    cvec = jnp.concatenate([c, c_ctx[None, :], jnp.zeros((pad_rows, d), F32)], axis=0)
    mod = _mod_call(cvec, w_mod[0], b_mod[0]).reshape(cvec.shape[0], N_MOD, d)

    w1_in = ffn1_w_in[0].astype(BF16)
    w1_out = ffn1_w_out[0].astype(BF16)
    w2_in = ffn2_w_in[0].astype(BF16)
    w2_out = ffn2_w_out[0].astype(BF16)
    w_in_b = w_in[0].astype(BF16)
    w_bc = w_branch_conv[0].astype(BF16)
    w_ba = w_branch_attn[0].astype(BF16)
    w_o = w_out[0].astype(BF16)

    tm = 512
    x1 = _ffn_call(x, mod, lambda b: b, norm1_g[0], w1_in, w1_out, mod_base=0, tm=tm, n_sub=2,
                   name="ffn1_latent")
    cx1 = _ffn_call(ctx, mod, lambda b: ctx_row, norm1_g[0], w1_in, w1_out, mod_base=0,
                    tm=ctx.shape[1], n_sub=1, name="ffn1_context")

    cos_t, sin_t = _rope_tables(seq)
    ck = ctx.shape[1]
    u, bg, gt, qt, k_l, vt_l = _proj_call(x1, mod, norm2_g[0], w_in_b, cos_t, sin_t, q_norm_g[0],
                                          k_norm_g[0], splits=splits, tm=tm, ck=ck, n_sub=2)
    k_c, vt_c = _proj_ctx_call(cx1, mod, ctx_row, norm2_g[0], w_in_b, k_norm_g[0], splits=splits)
    o = _attn_call(qt, k_c, k_l, vt_c, vt_l, tq=512, qb=256, unroll=16, lag=5)

    return _merge_call(x1, u, bg, gt, o, mod, conv_w[0], w_bc, w_ba, w_o, norm3_g[0], w2_in, w2_out,
                       final_g, tm=256, n_sub=2)
```

```python
import functools
import math

import jax
import jax.numpy as jnp
import numpy as np
from jax import lax
from jax.experimental import pallas as pl
from jax.experimental.pallas import tpu as pltpu

F32 = jnp.float32
BF16 = jnp.bfloat16

HEAD_DIM = 128
N_Q_HEADS = 8
N_KV_HEADS = 2
GROUP = N_Q_HEADS // N_KV_HEADS
GRID_W = 64
ROPE_THETA = 10000.0
EPS = 1e-6
N_MOD = 9
CONV_W = 3

V7X_VMEM_BYTES = 64 * 1024 * 1024
VMEM_LIMIT = V7X_VMEM_BYTES - 8 * 1024 * 1024
SUBLANES = 8
BF16_SUBLANES = 16

LOG2E = 1.4426950408889634
MAX_JUMP = 40.0


def _const_spec(shape):
    nd = len(shape)
    return pl.BlockSpec(shape, lambda *_: (0,) * nd, pipeline_mode=pl.Buffered(1))


def _rms(x, g):
    return x * lax.rsqrt(jnp.mean(x * x, axis=-1, keepdims=True) + EPS) * g


def _silu(a):
    return a * jax.nn.sigmoid(a)


def _dot(a, b):
    return jnp.dot(a, b, preferred_element_type=F32)


def _mod_kernel(c_ref, w_ref, b_ref, o_ref):
    s = _silu(c_ref[...])
    w = w_ref[...]
    s_hi = s.astype(BF16)
    s_lo = (s - s_hi.astype(F32)).astype(BF16)
    w_hi = w.astype(BF16)
    w_lo = (w - w_hi.astype(F32)).astype(BF16)
    o_ref[...] = _dot(s_hi, w_hi) + (_dot(s_lo, w_hi) + _dot(s_hi, w_lo)) + b_ref[...]


def _mod_call(cvec, w_mod, b_mod):
    rows, d = cvec.shape
    n = w_mod.shape[1]
    tn = d
    return pl.pallas_call(
        _mod_kernel,
        grid=(n // tn,),
        in_specs=[
            pl.BlockSpec((rows, d), lambda j: (0, 0)),
            pl.BlockSpec((d, tn), lambda j: (0, j)),
            pl.BlockSpec((1, tn), lambda j: (0, j)),
        ],
        out_specs=pl.BlockSpec((rows, tn), lambda j: (0, j)),
        out_shape=jax.ShapeDtypeStruct((rows, n), F32),
        name="mod",
    )(cvec, w_mod, b_mod.reshape(1, n))


def _row_split(tm, n_sub):
    assert tm % n_sub == 0
    return [slice(r * (tm // n_sub), (r + 1) * (tm // n_sub)) for r in range(n_sub)]


def _ffn_body(xs, shift, scale, gate, g, win_ref, wout_ref, d_ff):
    hbs = [(_rms(x, g) * (1.0 + scale) + shift).astype(BF16) for x in xs]
    acts = []
    for hb in hbs:
        a = _dot(hb, win_ref[:, :d_ff])
        b = _dot(hb, win_ref[:, d_ff:])
        acts.append((_silu(a) * b).astype(BF16))
    return [x + (0.5 * gate) * _dot(act, wout_ref[...]) for x, act in zip(xs, acts)]


def _ffn_kernel(x_ref, mod_ref, g_ref, win_ref, wout_ref, o_ref, *, mod_base, d_ff, n_sub):
    m = mod_ref[0]
    rows = _row_split(x_ref.shape[1], n_sub)
    outs = _ffn_body([x_ref[0, r, :] for r in rows], m[mod_base:mod_base + 1],
                     m[mod_base + 1:mod_base + 2], m[mod_base + 2:mod_base + 3], g_ref[...],
                     win_ref, wout_ref, d_ff)
    for r, out in zip(rows, outs):
        o_ref[0, r, :] = out


def _ffn_call(x, mod, mod_row, g, w_in, w_out, *, mod_base, tm, n_sub, name):
    bsz, seq, d = x.shape
    d_ff = w_out.shape[0]
    kern = functools.partial(_ffn_kernel, mod_base=mod_base, d_ff=d_ff, n_sub=n_sub)
    return pl.pallas_call(
        kern,
        grid=(bsz, seq // tm),
        in_specs=[
            pl.BlockSpec((1, tm, d), lambda b, i: (b, i, 0)),
            pl.BlockSpec((1, N_MOD, d), lambda b, i: (mod_row(b), 0, 0)),
            _const_spec((1, d)),
            _const_spec(w_in.shape),
            _const_spec(w_out.shape),
        ],
        out_specs=pl.BlockSpec((1, tm, d), lambda b, i: (b, i, 0)),
        out_shape=jax.ShapeDtypeStruct(x.shape, F32),
        compiler_params=pltpu.CompilerParams(
            dimension_semantics=("parallel", "parallel"), vmem_limit_bytes=VMEM_LIMIT),
        name=name,
    )(x, mod, g.reshape(1, d), w_in, w_out)


def _rope_rot(x):
    n = x.shape[-1]
    lane = lax.broadcasted_iota(jnp.int32, x.shape, x.ndim - 1)
    quarter = HEAD_DIM // 4
    fwd = pltpu.roll(x, n - quarter, x.ndim - 1)
    bwd = pltpu.roll(x, quarter, x.ndim - 1)
    return jnp.where((lane % (2 * quarter)) < quarter, fwd, bwd)


def _head_norm(xh, g):
    return xh * lax.rsqrt(jnp.mean(xh * xh, axis=-1, keepdims=True) + EPS) * g


def _proj_kernel(x_ref, mod_ref, g_ref, w_ref, cos_ref, sin_ref, qg_ref, kg_ref,
                 u_ref, bg_ref, gt_ref, qt_ref, k_ref, vt_ref, *, splits, n_sub):
    m = mod_ref[0]
    s_b, s_c, s_v, s_q, s_k, s_vv = splits
    n_in = w_ref.shape[1]
    ck = vt_ref.shape[-1]
    qg = qg_ref[...] * ((HEAD_DIM ** -0.5) * LOG2E)
    rows = _row_split(x_ref.shape[1], n_sub)
    hbs = [(_rms(x_ref[0, r, :], g_ref[...]) * (1.0 + m[4:5]) + m[3:4]).astype(BF16) for r in rows]

    for r, hb in zip(rows, hbs):
        cos = cos_ref[r, :]
        sin = sin_ref[r, :]
        q = _dot(hb, w_ref[:, s_v:s_q])
        k = _dot(hb, w_ref[:, s_q:s_k])
        for hd in range(N_Q_HEADS):
            qh = _head_norm(q[:, hd * HEAD_DIM:(hd + 1) * HEAD_DIM], qg)
            qt_ref[0, hd, :, r] = (qh * cos + _rope_rot(qh) * sin).T.astype(BF16)
        for hd in range(N_KV_HEADS):
            sl = slice(hd * HEAD_DIM, (hd + 1) * HEAD_DIM)
            kh = _head_norm(k[:, sl], kg_ref[...])
            k_ref[0, r, sl] = (kh * cos + _rope_rot(kh) * sin).astype(BF16)
        gt_ref[0, r, :] = jax.nn.sigmoid(_dot(hb, w_ref[:, s_vv:n_in]))
        v = _dot(hb, w_ref[:, s_k:s_vv])
        for hd in range(N_KV_HEADS):
            for c in range((r.stop - r.start) // ck):
                vc = v[c * ck:(c + 1) * ck, hd * HEAD_DIM:(hd + 1) * HEAD_DIM]
                vt_ref[0, hd, r.start // ck + c] = vc.T.astype(BF16)
        u_ref[0, r, :] = _dot(hb, w_ref[:, s_b:s_c]) * _dot(hb, w_ref[:, s_c:s_v])
        bg_ref[0, r, :] = _dot(hb, w_ref[:, :s_b])


def _proj_call(x, mod, g, w_in, cos_t, sin_t, q_g, k_g, *, splits, tm, ck, n_sub):
    bsz, seq, d = x.shape
    d_conv = splits[0]
    kv_w = splits[4] - splits[3]
    gates_w = w_in.shape[1] - splits[5]
    assert (tm // n_sub) % ck == 0
    kern = functools.partial(_proj_kernel, splits=splits, n_sub=n_sub)
    tok = lambda w: pl.BlockSpec((1, tm, w), lambda b, i: (b, i, 0))
    return pl.pallas_call(
        kern,
        grid=(bsz, seq // tm),
        in_specs=[
            tok(d),
            pl.BlockSpec((1, N_MOD, d), lambda b, i: (b, 0, 0)),
            _const_spec((1, d)),
            _const_spec(w_in.shape),
            pl.BlockSpec((tm, HEAD_DIM), lambda b, i: (i, 0)),
            pl.BlockSpec((tm, HEAD_DIM), lambda b, i: (i, 0)),
            _const_spec((1, HEAD_DIM)),
            _const_spec((1, HEAD_DIM)),
        ],
        out_specs=[
            tok(d_conv), tok(d_conv), tok(gates_w),
            pl.BlockSpec((1, N_Q_HEADS, HEAD_DIM, tm), lambda b, i: (b, 0, 0, i)),
            tok(kv_w),
            pl.BlockSpec((1, N_KV_HEADS, tm // ck, HEAD_DIM, ck), lambda b, i: (b, 0, i, 0, 0)),
        ],
        out_shape=[
            jax.ShapeDtypeStruct((bsz, seq, d_conv), F32),
            jax.ShapeDtypeStruct((bsz, seq, d_conv), F32),
            jax.ShapeDtypeStruct((bsz, seq, gates_w), F32),
            jax.ShapeDtypeStruct((bsz, N_Q_HEADS, HEAD_DIM, seq), BF16),
            jax.ShapeDtypeStruct((bsz, seq, kv_w), BF16),
            jax.ShapeDtypeStruct((bsz, N_KV_HEADS, seq // ck, HEAD_DIM, ck), BF16),
        ],
        compiler_params=pltpu.CompilerParams(
            dimension_semantics=("parallel", "parallel"), vmem_limit_bytes=VMEM_LIMIT),
        name="proj_latent",
    )(x, mod, g.reshape(1, d), w_in, cos_t, sin_t, q_g.reshape(1, HEAD_DIM), k_g.reshape(1, HEAD_DIM))


def _proj_ctx_kernel(x_ref, mod_ref, g_ref, w_ref, kg_ref, k_ref, vt_ref, *, kv_w):
    x = x_ref[0]
    m = mod_ref[0]
    h = _rms(x, g_ref[...]) * (1.0 + m[4:5]) + m[3:4]
    hb = h.astype(BF16)
    k = _dot(hb, w_ref[:, :kv_w])
    v = _dot(hb, w_ref[:, kv_w:])
    for hd in range(N_KV_HEADS):
        sl = slice(hd * HEAD_DIM, (hd + 1) * HEAD_DIM)
        k_ref[0, :, sl] = _head_norm(k[:, sl], kg_ref[...]).astype(BF16)
        vt_ref[0, hd] = v[:, sl].T.astype(BF16)


def _proj_ctx_call(cx, mod, mod_row, g, w_in, k_g, *, splits):
    bsz, seq, d = cx.shape
    kv_w = splits[4] - splits[3]
    col_blk = splits[3] // (2 * kv_w)
    assert col_blk * 2 * kv_w == splits[3]
    kern = functools.partial(_proj_ctx_kernel, kv_w=kv_w)
    return pl.pallas_call(
        kern,
        grid=(bsz,),
        in_specs=[
            pl.BlockSpec((1, seq, d), lambda b: (b, 0, 0)),
            pl.BlockSpec((1, N_MOD, d), lambda b: (mod_row, 0, 0)),
            _const_spec((1, d)),
            pl.BlockSpec((d, 2 * kv_w), lambda b: (0, col_blk)),
            _const_spec((1, HEAD_DIM)),
        ],
        out_specs=[
            pl.BlockSpec((1, seq, kv_w), lambda b: (b, 0, 0)),
            pl.BlockSpec((1, N_KV_HEADS, HEAD_DIM, seq), lambda b: (b, 0, 0, 0)),
        ],
        out_shape=[
            jax.ShapeDtypeStruct((bsz, seq, kv_w), BF16),
            jax.ShapeDtypeStruct((bsz, N_KV_HEADS, HEAD_DIM, seq), BF16),
        ],
        compiler_params=pltpu.CompilerParams(dimension_semantics=("parallel",)),
        name="proj_context",
    )(cx, mod, g.reshape(1, d), w_in, k_g.reshape(1, HEAD_DIM))


def _attn_kernel(qt_ref, kc_ref, kl_ref, vtc_ref, vtl_ref, o_ref, ref_ref, max_ref, jump_ref, acc_ref,
                 *, qb, unroll, lag):
    d, tq = qt_ref.shape[2:]
    ck = vtl_ref.shape[-1]
    n_lat = vtl_ref.shape[2]
    blocks = [(g, t) for g in range(GROUP) for t in range(tq // qb)]

    def scores(k, b):
        g, t = blocks[b]
        return _dot(k(), qt_ref[0, g, :, t * qb:(t + 1) * qb])

    def accumulate(vt, b, p, alpha):
        vt_ext = jnp.concatenate([vt(), jnp.ones((BF16_SUBLANES, p.shape[0]), BF16)], axis=0)
        acc_ref[b] = alpha * acc_ref[b] + _dot(vt_ext, p)

    def update_exact(vt, b, s):
        r_old = ref_ref[b]
        r = jnp.maximum(r_old, jnp.max(s, axis=0, keepdims=True))
        accumulate(vt, b, jnp.exp2(s - r).astype(BF16), jnp.exp2(r_old - r))
        ref_ref[b] = r
        max_ref[b] = r

    def update_stale(vt, b, s):
        r = max_ref[b]
        new_max = jnp.maximum(r, jnp.max(s, axis=0, keepdims=True))
        accumulate(vt, b, jnp.exp2(s - r).astype(BF16), jnp.exp2(ref_ref[b] - r))
        jump_ref[b] = jnp.maximum(jump_ref[b], new_max - r)
        ref_ref[b] = r
        max_ref[b] = new_max

    def run(chunks, update, lag):
        items = [(k, vt, b) for k, vt in chunks for b in range(len(blocks))]
        pending = {}
        for i in range(len(items) + lag):
            if i < len(items):
                k, _, b = items[i]
                pending[i] = scores(k, b)
            if i >= lag:
                _, vt, b = items[i - lag]
                update(vt, b, pending.pop(i - lag))

    def latent_chunk(j):
        off = pl.multiple_of(j * ck, ck)
        return (lambda: kl_ref[0, pl.ds(off, ck), :]), (lambda: vtl_ref[0, 0, j])

    def attend(update_latent, unroll, lag):
        ref_ref[...] = jnp.full(ref_ref.shape, -jnp.inf, F32)
        acc_ref[...] = jnp.zeros(acc_ref.shape, F32)
        run([((lambda: kc_ref[0]), (lambda: vtc_ref[0, 0]))], update_exact, lag)

        def body(it, carry):
            run([latent_chunk(it * unroll + u) for u in range(unroll)], update_latent, lag)
            return carry

        lax.fori_loop(0, n_lat // unroll, body, 0)

    jump_ref[...] = jnp.zeros(jump_ref.shape, F32)
    attend(update_stale, unroll, lag)

    @pl.when(jnp.max(jump_ref[...]) > MAX_JUMP)
    def _():
        attend(update_exact, 1, 1)

    for b, (g, t) in enumerate(blocks):
        acc = acc_ref[b]
        o = (acc[:d] / acc[d:d + 1]).T
        o_ref[0, t * qb:(t + 1) * qb, g * d:(g + 1) * d] = o.astype(BF16)


def _attn_call(qt, k_c, k_l, vt_c, vt_l, *, tq, qb, unroll, lag):
    bsz, n_heads, d, seq = qt.shape
    n_lat, _, ck = vt_l.shape[2:]
    assert k_c.shape[1] == ck and vt_c.shape[-1] == ck
    assert tq % qb == 0 and n_lat % unroll == 0
    n_blocks = GROUP * (tq // qb)
    kern = functools.partial(_attn_kernel, qb=qb, unroll=unroll, lag=lag)
    stat = pltpu.VMEM((n_blocks, 1, qb), F32)
    return pl.pallas_call(
        kern,
        grid=(bsz, N_KV_HEADS, seq // tq),
        in_specs=[
            pl.BlockSpec((1, GROUP, d, tq), lambda b, h, i: (b, h, 0, i)),
            pl.BlockSpec((1, ck, d), lambda b, h, i: (b, 0, h)),
            pl.BlockSpec((1, seq, d), lambda b, h, i: (b, 0, h)),
            pl.BlockSpec((1, 1, d, ck), lambda b, h, i: (b, h, 0, 0)),
            pl.BlockSpec((1, 1, n_lat, d, ck), lambda b, h, i: (b, h, 0, 0, 0)),
        ],
        out_specs=pl.BlockSpec((1, tq, GROUP * d), lambda b, h, i: (b, i, h)),
        out_shape=jax.ShapeDtypeStruct((bsz, seq, n_heads * d), BF16),
        scratch_shapes=[stat, stat, stat, pltpu.VMEM((n_blocks, d + BF16_SUBLANES, qb), F32)],
        compiler_params=pltpu.CompilerParams(
            dimension_semantics=("parallel", "parallel", "arbitrary"), vmem_limit_bytes=VMEM_LIMIT),
        name="attention",
    )(qt, k_c, k_l, vt_c, vt_l)


def _merge_kernel(x_ref, u_ref, up_ref, un_ref, bg_ref, gt_ref, o_ref, mod_ref, cw_ref,
                  wbc_ref, wba_ref, wout_ref, g3_ref, win2_ref, wout2_ref, fg_ref, out_ref,
                  *, tm, d_ff, n_sub):
    i = pl.program_id(1)
    last = pl.num_programs(1) - 1
    x = x_ref[0]
    m = mod_ref[0]
    u = u_ref[0]
    d = x.shape[-1]

    row = lax.broadcasted_iota(jnp.int32, (tm, 1), 0)
    prev_edge = jnp.where(i > 0, up_ref[0, SUBLANES - 1:SUBLANES, :], 0.0)
    next_edge = jnp.where(i < last, un_ref[0, 0:1, :], 0.0)
    u_prev = jnp.where(row == 0, prev_edge, pltpu.roll(u, 1, 0))
    u_next = jnp.where(row == tm - 1, next_edge, pltpu.roll(u, tm - 1, 0))
    cw = cw_ref[...]
    y_conv = bg_ref[0] * (u_prev * cw[0:1] + u * cw[1:2] + u_next * cw[2:3])

    rows = _row_split(tm, n_sub)
    yb = y_conv.astype(BF16)
    merged = [(gt_ref[0, r, :d] * _dot(yb[r], wbc_ref[...])
               + gt_ref[0, r, d:] * _dot(o_ref[0, r, :], wba_ref[...])).astype(BF16) for r in rows]
    xs = [x[r] + m[5:6] * _dot(mg, wout_ref[...]) for r, mg in zip(rows, merged)]
    xs = _ffn_body(xs, m[6:7], m[7:8], m[8:9], g3_ref[...], win2_ref, wout2_ref, d_ff)
    for r, xr in zip(rows, xs):
        out_ref[0, r, :] = _rms(xr, fg_ref[...])


def _merge_call(x, u, bg, gt, o, mod, conv_w, w_bc, w_ba, w_out, g3, w_in2, w_out2, fg, *, tm, n_sub):
    bsz, seq, d = x.shape
    d_ff = w_out2.shape[0]
    hb = tm // SUBLANES
    n_halo = seq // SUBLANES
    kern = functools.partial(_merge_kernel, tm=tm, d_ff=d_ff, n_sub=n_sub)
    tok = lambda w: pl.BlockSpec((1, tm, w), lambda b, i: (b, i, 0))
    return pl.pallas_call(
        kern,
        grid=(bsz, seq // tm),
        in_specs=[
            tok(d),
            tok(u.shape[-1]),
            pl.BlockSpec((1, SUBLANES, u.shape[-1]), lambda b, i: (b, jnp.maximum(i * hb - 1, 0), 0)),
            pl.BlockSpec((1, SUBLANES, u.shape[-1]),
                         lambda b, i: (b, jnp.minimum((i + 1) * hb, n_halo - 1), 0)),
            tok(bg.shape[-1]),
            tok(gt.shape[-1]),
            tok(o.shape[-1]),
            pl.BlockSpec((1, N_MOD, d), lambda b, i: (b, 0, 0)),
            _const_spec(conv_w.shape),
            _const_spec(w_bc.shape),
            _const_spec(w_ba.shape),
            _const_spec(w_out.shape),
            _const_spec((1, d)),
            _const_spec(w_in2.shape),
            _const_spec(w_out2.shape),
            _const_spec((1, d)),
        ],
        out_specs=tok(d),
        out_shape=jax.ShapeDtypeStruct(x.shape, F32),
        compiler_params=pltpu.CompilerParams(
            dimension_semantics=("parallel", "parallel"), vmem_limit_bytes=VMEM_LIMIT),
        name="merge_ffn2",
    )(x, u, u, u, bg, gt, o, mod, conv_w, w_bc, w_ba, w_out, g3.reshape(1, d), w_in2, w_out2,
      fg.reshape(1, d))


def _rope_tables(seq):
    f32 = np.float32
    pos = np.arange(seq)
    n_freq = HEAD_DIM // 4
    inv = np.power(f32(ROPE_THETA), -np.arange(n_freq, dtype=f32) / f32(n_freq)).astype(f32)
    ang_r = (pos // GRID_W).astype(f32)[:, None] * inv
    ang_c = (pos % GRID_W).astype(f32)[:, None] * inv
    cos_r, sin_r, cos_c, sin_c = (fn(a.astype(np.float64)).astype(f32)
                                  for a in (ang_r, ang_c) for fn in (np.cos, np.sin))
    cos_t = np.concatenate([cos_r, cos_r, cos_c, cos_c], axis=-1)
    sin_t = np.concatenate([-sin_r, sin_r, -sin_c, sin_c], axis=-1)
    return jnp.asarray(cos_t), jnp.asarray(sin_t)


def kernel(x, c, ctx, c_ctx, w_mod, b_mod, norm1_g, norm2_g, norm3_g, ffn1_w_in, ffn1_w_out,
           w_in, conv_w, q_norm_g, k_norm_g, w_branch_conv, w_branch_attn, w_out, ffn2_w_in,
           ffn2_w_out, final_g):
    bsz, seq, d = x.shape
    depth = w_mod.shape[0]
    assert depth == 1, "single-layer problem"
    d_conv = conv_w.shape[-1]
    q_w = N_Q_HEADS * HEAD_DIM
    kv_w = N_KV_HEADS * HEAD_DIM
    widths = (d_conv, d_conv, d_conv, q_w, kv_w, kv_w)
    splits = tuple(int(sum(widths[:j + 1])) for j in range(len(widths)))
    assert w_in.shape[-1] == splits[-1] + 2 * d

    ctx_row = bsz
    pad_rows = -(bsz + 1) % SUBLANES
    cvec = jnp.concatenate([c, c_ctx[None, :], jnp.zeros((pad_rows, d), F32)], axis=0)
    mod = _mod_call(cvec, w_mod[0], b_mod[0]).reshape(cvec.shape[0], N_MOD, d)

    w1_in = ffn1_w_in[0].astype(BF16)
    w1_out = ffn1_w_out[0].astype(BF16)
    w2_in = ffn2_w_in[0].astype(BF16)
    w2_out = ffn2_w_out[0].astype(BF16)
    w_in_b = w_in[0].astype(BF16)
    w_bc = w_branch_conv[0].astype(BF16)
    w_ba = w_branch_attn[0].astype(BF16)
    w_o = w_out[0].astype(BF16)

    tm = 512
    x1 = _ffn_call(x, mod, lambda b: b, norm1_g[0], w1_in, w1_out, mod_base=0, tm=tm, n_sub=2,
                   name="ffn1_latent")
    cx1 = _ffn_call(ctx, mod, lambda b: ctx_row, norm1_g[0], w1_in, w1_out, mod_base=0,
                    tm=ctx.shape[1], n_sub=1, name="ffn1_context")

    cos_t, sin_t = _rope_tables(seq)
    ck = ctx.shape[1]
    u, bg, gt, qt, k_l, vt_l = _proj_call(x1, mod, norm2_g[0], w_in_b, cos_t, sin_t, q_norm_g[0],
                                          k_norm_g[0], splits=splits, tm=tm, ck=ck, n_sub=2)
    k_c, vt_c = _proj_ctx_call(cx1, mod, ctx_row, norm2_g[0], w_in_b, k_norm_g[0], splits=splits)
    o = _attn_call(qt, k_c, k_l, vt_c, vt_l, tq=512, qb=256, unroll=16, lag=4)

    return _merge_call(x1, u, bg, gt, o, mod, conv_w[0], w_bc, w_ba, w_o, norm3_g[0], w2_in, w2_out,
                       final_g, tm=256, n_sub=2)
```

```python
import functools
import math

import jax
import jax.numpy as jnp
import numpy as np
from jax import lax
from jax.experimental import pallas as pl
from jax.experimental.pallas import tpu as pltpu

F32 = jnp.float32
BF16 = jnp.bfloat16

HEAD_DIM = 128
N_Q_HEADS = 8
N_KV_HEADS = 2
GROUP = N_Q_HEADS // N_KV_HEADS
GRID_W = 64
ROPE_THETA = 10000.0
EPS = 1e-6
N_MOD = 9
CONV_W = 3

V7X_VMEM_BYTES = 64 * 1024 * 1024
VMEM_LIMIT = V7X_VMEM_BYTES - 8 * 1024 * 1024
SUBLANES = 8
BF16_SUBLANES = 16

LOG2E = 1.4426950408889634
MAX_JUMP = 40.0


def _const_spec(shape):
    nd = len(shape)
    return pl.BlockSpec(shape, lambda *_: (0,) * nd, pipeline_mode=pl.Buffered(1))


def _rms(x, g):
    return x * lax.rsqrt(jnp.mean(x * x, axis=-1, keepdims=True) + EPS) * g


def _silu(a):
    return a * jax.nn.sigmoid(a)


def _dot(a, b):
    return jnp.dot(a, b, preferred_element_type=F32)


def _mod_kernel(c_ref, w_ref, b_ref, o_ref):
    s = _silu(c_ref[...])
    w = w_ref[...]
    s_hi = s.astype(BF16)
    s_lo = (s - s_hi.astype(F32)).astype(BF16)
    w_hi = w.astype(BF16)
    w_lo = (w - w_hi.astype(F32)).astype(BF16)
    o_ref[...] = _dot(s_hi, w_hi) + (_dot(s_lo, w_hi) + _dot(s_hi, w_lo)) + b_ref[...]


def _mod_call(cvec, w_mod, b_mod):
    rows, d = cvec.shape
    n = w_mod.shape[1]
    tn = d
    return pl.pallas_call(
        _mod_kernel,
        grid=(n // tn,),
        in_specs=[
            pl.BlockSpec((rows, d), lambda j: (0, 0)),
            pl.BlockSpec((d, tn), lambda j: (0, j)),
            pl.BlockSpec((1, tn), lambda j: (0, j)),
        ],
        out_specs=pl.BlockSpec((rows, tn), lambda j: (0, j)),
        out_shape=jax.ShapeDtypeStruct((rows, n), F32),
        name="mod",
    )(cvec, w_mod, b_mod.reshape(1, n))


def _row_split(tm, n_sub):
    assert tm % n_sub == 0
    return [slice(r * (tm // n_sub), (r + 1) * (tm // n_sub)) for r in range(n_sub)]


def _ffn_body(xs, shift, scale, gate, g, win_ref, wout_ref, d_ff):
    hbs = [(_rms(x, g) * (1.0 + scale) + shift).astype(BF16) for x in xs]
    acts = []
    for hb in hbs:
        a = _dot(hb, win_ref[:, :d_ff])
        b = _dot(hb, win_ref[:, d_ff:])
        acts.append((_silu(a) * b).astype(BF16))
    return [x + (0.5 * gate) * _dot(act, wout_ref[...]) for x, act in zip(xs, acts)]


def _ffn_kernel(x_ref, mod_ref, g_ref, win_ref, wout_ref, o_ref, *, mod_base, d_ff, n_sub):
    m = mod_ref[0]
    rows = _row_split(x_ref.shape[1], n_sub)
    outs = _ffn_body([x_ref[0, r, :] for r in rows], m[mod_base:mod_base + 1],
                     m[mod_base + 1:mod_base + 2], m[mod_base + 2:mod_base + 3], g_ref[...],
                     win_ref, wout_ref, d_ff)
    for r, out in zip(rows, outs):
        o_ref[0, r, :] = out


def _ffn_call(x, mod, mod_row, g, w_in, w_out, *, mod_base, tm, n_sub, name):
    bsz, seq, d = x.shape
    d_ff = w_out.shape[0]
    kern = functools.partial(_ffn_kernel, mod_base=mod_base, d_ff=d_ff, n_sub=n_sub)
    return pl.pallas_call(
        kern,
        grid=(bsz, seq // tm),
        in_specs=[
            pl.BlockSpec((1, tm, d), lambda b, i: (b, i, 0)),
            pl.BlockSpec((1, N_MOD, d), lambda b, i: (mod_row(b), 0, 0)),
            _const_spec((1, d)),
            _const_spec(w_in.shape),
            _const_spec(w_out.shape),
        ],
        out_specs=pl.BlockSpec((1, tm, d), lambda b, i: (b, i, 0)),
        out_shape=jax.ShapeDtypeStruct(x.shape, F32),
        compiler_params=pltpu.CompilerParams(
            dimension_semantics=("parallel", "parallel"), vmem_limit_bytes=VMEM_LIMIT),
        name=name,
    )(x, mod, g.reshape(1, d), w_in, w_out)


def _rope_rot(x):
    n = x.shape[-1]
    lane = lax.broadcasted_iota(jnp.int32, x.shape, x.ndim - 1)
    quarter = HEAD_DIM // 4
    fwd = pltpu.roll(x, n - quarter, x.ndim - 1)
    bwd = pltpu.roll(x, quarter, x.ndim - 1)
    return jnp.where((lane % (2 * quarter)) < quarter, fwd, bwd)


def _head_norm(xh, g):
    return xh * lax.rsqrt(jnp.mean(xh * xh, axis=-1, keepdims=True) + EPS) * g


def _proj_kernel(x_ref, mod_ref, g_ref, w_ref, cos_ref, sin_ref, qg_ref, kg_ref,
                 u_ref, bg_ref, gt_ref, qt_ref, k_ref, vt_ref, *, splits, n_sub):
    m = mod_ref[0]
    s_b, s_c, s_v, s_q, s_k, s_vv = splits
    n_in = w_ref.shape[1]
    ck = vt_ref.shape[-1]
    qg = qg_ref[...] * ((HEAD_DIM ** -0.5) * LOG2E)
    rows = _row_split(x_ref.shape[1], n_sub)
    hbs = [(_rms(x_ref[0, r, :], g_ref[...]) * (1.0 + m[4:5]) + m[3:4]).astype(BF16) for r in rows]

    for r, hb in zip(rows, hbs):
        cos = cos_ref[r, :]
        sin = sin_ref[r, :]
        q = _dot(hb, w_ref[:, s_v:s_q])
        k = _dot(hb, w_ref[:, s_q:s_k])
        for hd in range(N_Q_HEADS):
            qh = _head_norm(q[:, hd * HEAD_DIM:(hd + 1) * HEAD_DIM], qg)
            qt_ref[0, hd, :, r] = (qh * cos + _rope_rot(qh) * sin).T.astype(BF16)
        for hd in range(N_KV_HEADS):
            sl = slice(hd * HEAD_DIM, (hd + 1) * HEAD_DIM)
            kh = _head_norm(k[:, sl], kg_ref[...])
            k_ref[0, r, sl] = (kh * cos + _rope_rot(kh) * sin).astype(BF16)
        gt_ref[0, r, :] = jax.nn.sigmoid(_dot(hb, w_ref[:, s_vv:n_in])).astype(BF16)
        v = _dot(hb, w_ref[:, s_k:s_vv])
        for hd in range(N_KV_HEADS):
            for c in range((r.stop - r.start) // ck):
                vc = v[c * ck:(c + 1) * ck, hd * HEAD_DIM:(hd + 1) * HEAD_DIM]
                vt_ref[0, hd, r.start // ck + c] = vc.T.astype(BF16)
        u_ref[0, r, :] = (_dot(hb, w_ref[:, s_b:s_c]) * _dot(hb, w_ref[:, s_c:s_v])).astype(BF16)
        bg_ref[0, r, :] = _dot(hb, w_ref[:, :s_b]).astype(BF16)


def _proj_call(x, mod, g, w_in, cos_t, sin_t, q_g, k_g, *, splits, tm, ck, n_sub):
    bsz, seq, d = x.shape
    d_conv = splits[0]
    kv_w = splits[4] - splits[3]
    gates_w = w_in.shape[1] - splits[5]
    assert (tm // n_sub) % ck == 0
    kern = functools.partial(_proj_kernel, splits=splits, n_sub=n_sub)
    tok = lambda w: pl.BlockSpec((1, tm, w), lambda b, i: (b, i, 0))
    return pl.pallas_call(
        kern,
        grid=(bsz, seq // tm),
        in_specs=[
            tok(d),
            pl.BlockSpec((1, N_MOD, d), lambda b, i: (b, 0, 0)),
            _const_spec((1, d)),
            _const_spec(w_in.shape),
            pl.BlockSpec((tm, HEAD_DIM), lambda b, i: (i, 0)),
            pl.BlockSpec((tm, HEAD_DIM), lambda b, i: (i, 0)),
            _const_spec((1, HEAD_DIM)),
            _const_spec((1, HEAD_DIM)),
        ],
        out_specs=[
            tok(d_conv), tok(d_conv), tok(gates_w),
            pl.BlockSpec((1, N_Q_HEADS, HEAD_DIM, tm), lambda b, i: (b, 0, 0, i)),
            tok(kv_w),
            pl.BlockSpec((1, N_KV_HEADS, tm // ck, HEAD_DIM, ck), lambda b, i: (b, 0, i, 0, 0)),
        ],
        out_shape=[
            jax.ShapeDtypeStruct((bsz, seq, d_conv), BF16),
            jax.ShapeDtypeStruct((bsz, seq, d_conv), BF16),
            jax.ShapeDtypeStruct((bsz, seq, gates_w), BF16),
            jax.ShapeDtypeStruct((bsz, N_Q_HEADS, HEAD_DIM, seq), BF16),
            jax.ShapeDtypeStruct((bsz, seq, kv_w), BF16),
            jax.ShapeDtypeStruct((bsz, N_KV_HEADS, seq // ck, HEAD_DIM, ck), BF16),
        ],
        compiler_params=pltpu.CompilerParams(
            dimension_semantics=("parallel", "parallel"), vmem_limit_bytes=VMEM_LIMIT),
        name="proj_latent",
    )(x, mod, g.reshape(1, d), w_in, cos_t, sin_t, q_g.reshape(1, HEAD_DIM), k_g.reshape(1, HEAD_DIM))


def _proj_ctx_kernel(x_ref, mod_ref, g_ref, w_ref, kg_ref, k_ref, vt_ref, *, kv_w):
    x = x_ref[0]
    m = mod_ref[0]
    h = _rms(x, g_ref[...]) * (1.0 + m[4:5]) + m[3:4]
    hb = h.astype(BF16)
    k = _dot(hb, w_ref[:, :kv_w])
    v = _dot(hb, w_ref[:, kv_w:])
    for hd in range(N_KV_HEADS):
        sl = slice(hd * HEAD_DIM, (hd + 1) * HEAD_DIM)
        k_ref[0, :, sl] = _head_norm(k[:, sl], kg_ref[...]).astype(BF16)
        vt_ref[0, hd] = v[:, sl].T.astype(BF16)


def _proj_ctx_call(cx, mod, mod_row, g, w_in, k_g, *, splits):
    bsz, seq, d = cx.shape
    kv_w = splits[4] - splits[3]
    col_blk = splits[3] // (2 * kv_w)
    assert col_blk * 2 * kv_w == splits[3]
    kern = functools.partial(_proj_ctx_kernel, kv_w=kv_w)
    return pl.pallas_call(
        kern,
        grid=(bsz,),
        in_specs=[
            pl.BlockSpec((1, seq, d), lambda b: (b, 0, 0)),
            pl.BlockSpec((1, N_MOD, d), lambda b: (mod_row, 0, 0)),
            _const_spec((1, d)),
            pl.BlockSpec((d, 2 * kv_w), lambda b: (0, col_blk)),
            _const_spec((1, HEAD_DIM)),
        ],
        out_specs=[
            pl.BlockSpec((1, seq, kv_w), lambda b: (b, 0, 0)),
            pl.BlockSpec((1, N_KV_HEADS, HEAD_DIM, seq), lambda b: (b, 0, 0, 0)),
        ],
        out_shape=[
            jax.ShapeDtypeStruct((bsz, seq, kv_w), BF16),
            jax.ShapeDtypeStruct((bsz, N_KV_HEADS, HEAD_DIM, seq), BF16),
        ],
        compiler_params=pltpu.CompilerParams(dimension_semantics=("parallel",)),
        name="proj_context",
    )(cx, mod, g.reshape(1, d), w_in, k_g.reshape(1, HEAD_DIM))


def _attn_kernel(qt_ref, kc_ref, kl_ref, vtc_ref, vtl_ref, o_ref, ref_ref, max_ref, jump_ref, acc_ref,
                 *, qb, unroll, lag):
    d, tq = qt_ref.shape[2:]
    ck = vtl_ref.shape[-1]
    n_lat = vtl_ref.shape[2]
    blocks = [(g, t) for g in range(GROUP) for t in range(tq // qb)]

    def scores(k, b):
        g, t = blocks[b]
        return _dot(k(), qt_ref[0, g, :, t * qb:(t + 1) * qb])

    def accumulate(vt, b, p, alpha):
        vt_ext = jnp.concatenate([vt(), jnp.ones((BF16_SUBLANES, p.shape[0]), BF16)], axis=0)
        acc_ref[b] = alpha * acc_ref[b] + _dot(vt_ext, p)

    def update_exact(vt, b, s):
        r_old = ref_ref[b]
        r = jnp.maximum(r_old, jnp.max(s, axis=0, keepdims=True))
        accumulate(vt, b, jnp.exp2(s - r).astype(BF16), jnp.exp2(r_old - r))
        ref_ref[b] = r
        max_ref[b] = r

    def update_stale(vt, b, s):
        r = max_ref[b]
        new_max = jnp.maximum(r, jnp.max(s, axis=0, keepdims=True))
        accumulate(vt, b, jnp.exp2(s - r).astype(BF16), jnp.exp2(ref_ref[b] - r))
        jump_ref[b] = jnp.maximum(jump_ref[b], new_max - r)
        ref_ref[b] = r
        max_ref[b] = new_max

    def run(chunks, update, lag):
        items = [(k, vt, b) for k, vt in chunks for b in range(len(blocks))]
        pending = {}
        for i in range(len(items) + lag):
            if i < len(items):
                k, _, b = items[i]
                pending[i] = scores(k, b)
            if i >= lag:
                _, vt, b = items[i - lag]
                update(vt, b, pending.pop(i - lag))

    def latent_chunk(j):
        off = pl.multiple_of(j * ck, ck)
        return (lambda: kl_ref[0, pl.ds(off, ck), :]), (lambda: vtl_ref[0, 0, j])

    def attend(update_latent, unroll, lag):
        ref_ref[...] = jnp.full(ref_ref.shape, -jnp.inf, F32)
        acc_ref[...] = jnp.zeros(acc_ref.shape, F32)
        run([((lambda: kc_ref[0]), (lambda: vtc_ref[0, 0]))], update_exact, lag)

        def body(it, carry):
            run([latent_chunk(it * unroll + u) for u in range(unroll)], update_latent, lag)
            return carry

        lax.fori_loop(0, n_lat // unroll, body, 0)

    jump_ref[...] = jnp.zeros(jump_ref.shape, F32)
    attend(update_stale, unroll, lag)

    @pl.when(jnp.max(jump_ref[...]) > MAX_JUMP)
    def _():
        attend(update_exact, 1, 1)

    for b, (g, t) in enumerate(blocks):
        acc = acc_ref[b]
        o = (acc[:d] / acc[d:d + 1]).T
        o_ref[0, t * qb:(t + 1) * qb, g * d:(g + 1) * d] = o.astype(BF16)


def _attn_call(qt, k_c, k_l, vt_c, vt_l, *, tq, qb, unroll, lag):
    bsz, n_heads, d, seq = qt.shape
    n_lat, _, ck = vt_l.shape[2:]
    assert k_c.shape[1] == ck and vt_c.shape[-1] == ck
    assert tq % qb == 0 and n_lat % unroll == 0
    n_blocks = GROUP * (tq // qb)
    kern = functools.partial(_attn_kernel, qb=qb, unroll=unroll, lag=lag)
    stat = pltpu.VMEM((n_blocks, 1, qb), F32)
    return pl.pallas_call(
        kern,
        grid=(bsz, N_KV_HEADS, seq // tq),
        in_specs=[
            pl.BlockSpec((1, GROUP, d, tq), lambda b, h, i: (b, h, 0, i)),
            pl.BlockSpec((1, ck, d), lambda b, h, i: (b, 0, h)),
            pl.BlockSpec((1, seq, d), lambda b, h, i: (b, 0, h)),
            pl.BlockSpec((1, 1, d, ck), lambda b, h, i: (b, h, 0, 0)),
            pl.BlockSpec((1, 1, n_lat, d, ck), lambda b, h, i: (b, h, 0, 0, 0)),
        ],
        out_specs=pl.BlockSpec((1, tq, GROUP * d), lambda b, h, i: (b, i, h)),
        out_shape=jax.ShapeDtypeStruct((bsz, seq, n_heads * d), BF16),
        scratch_shapes=[stat, stat, stat, pltpu.VMEM((n_blocks, d + BF16_SUBLANES, qb), F32)],
        compiler_params=pltpu.CompilerParams(
            dimension_semantics=("parallel", "parallel", "arbitrary"), vmem_limit_bytes=VMEM_LIMIT),
        name="attention",
    )(qt, k_c, k_l, vt_c, vt_l)


def _merge_kernel(x_ref, u_ref, up_ref, un_ref, bg_ref, gt_ref, o_ref, mod_ref, cw_ref,
                  wbc_ref, wba_ref, wout_ref, g3_ref, win2_ref, wout2_ref, fg_ref, out_ref,
                  *, tm, d_ff, n_sub):
    i = pl.program_id(1)
    last = pl.num_programs(1) - 1
    x = x_ref[0]
    m = mod_ref[0]
    u = u_ref[0].astype(F32)
    d = x.shape[-1]

    row = lax.broadcasted_iota(jnp.int32, (tm, 1), 0)
    halo = up_ref.shape[1]
    prev_edge = jnp.where(i > 0, up_ref[0, halo - 1:halo, :].astype(F32), 0.0)
    next_edge = jnp.where(i < last, un_ref[0, 0:1, :].astype(F32), 0.0)
    u_prev = jnp.where(row == 0, prev_edge, pltpu.roll(u, 1, 0))
    u_next = jnp.where(row == tm - 1, next_edge, pltpu.roll(u, tm - 1, 0))
    cw = cw_ref[...]
    y_conv = bg_ref[0].astype(F32) * (u_prev * cw[0:1] + u * cw[1:2] + u_next * cw[2:3])

    rows = _row_split(tm, n_sub)
    yb = y_conv.astype(BF16)
    merged = [(gt_ref[0, r, :d].astype(F32) * _dot(yb[r], wbc_ref[...])
               + gt_ref[0, r, d:].astype(F32) * _dot(o_ref[0, r, :], wba_ref[...])).astype(BF16)
              for r in rows]
    xs = [x[r] + m[5:6] * _dot(mg, wout_ref[...]) for r, mg in zip(rows, merged)]
    xs = _ffn_body(xs, m[6:7], m[7:8], m[8:9], g3_ref[...], win2_ref, wout2_ref, d_ff)
    for r, xr in zip(rows, xs):
        out_ref[0, r, :] = _rms(xr, fg_ref[...])


def _merge_call(x, u, bg, gt, o, mod, conv_w, w_bc, w_ba, w_out, g3, w_in2, w_out2, fg, *, tm, n_sub):
    bsz, seq, d = x.shape
    d_ff = w_out2.shape[0]
    halo = BF16_SUBLANES if u.dtype == BF16 else SUBLANES
    hb = tm // halo
    n_halo = seq // halo
    kern = functools.partial(_merge_kernel, tm=tm, d_ff=d_ff, n_sub=n_sub)
    tok = lambda w: pl.BlockSpec((1, tm, w), lambda b, i: (b, i, 0))
    return pl.pallas_call(
        kern,
        grid=(bsz, seq // tm),
        in_specs=[
            tok(d),
            tok(u.shape[-1]),
            pl.BlockSpec((1, halo, u.shape[-1]), lambda b, i: (b, jnp.maximum(i * hb - 1, 0), 0)),
            pl.BlockSpec((1, halo, u.shape[-1]),
                         lambda b, i: (b, jnp.minimum((i + 1) * hb, n_halo - 1), 0)),
            tok(bg.shape[-1]),
            tok(gt.shape[-1]),
            tok(o.shape[-1]),
            pl.BlockSpec((1, N_MOD, d), lambda b, i: (b, 0, 0)),
            _const_spec(conv_w.shape),
            _const_spec(w_bc.shape),
            _const_spec(w_ba.shape),
            _const_spec(w_out.shape),
            _const_spec((1, d)),
            _const_spec(w_in2.shape),
            _const_spec(w_out2.shape),
            _const_spec((1, d)),
        ],
        out_specs=tok(d),
        out_shape=jax.ShapeDtypeStruct(x.shape, F32),
        compiler_params=pltpu.CompilerParams(
            dimension_semantics=("parallel", "parallel"), vmem_limit_bytes=VMEM_LIMIT),
        name="merge_ffn2",
    )(x, u, u, u, bg, gt, o, mod, conv_w, w_bc, w_ba, w_out, g3.reshape(1, d), w_in2, w_out2,
      fg.reshape(1, d))


def _rope_tables(seq):
    f32 = np.float32
    pos = np.arange(seq)
    n_freq = HEAD_DIM // 4
    inv = np.power(f32(ROPE_THETA), -np.arange(n_freq, dtype=f32) / f32(n_freq)).astype(f32)
    ang_r = (pos // GRID_W).astype(f32)[:, None] * inv
    ang_c = (pos % GRID_W).astype(f32)[:, None] * inv
    cos_r, sin_r, cos_c, sin_c = (fn(a.astype(np.float64)).astype(f32)
                                  for a in (ang_r, ang_c) for fn in (np.cos, np.sin))
    cos_t = np.concatenate([cos_r, cos_r, cos_c, cos_c], axis=-1)
    sin_t = np.concatenate([-sin_r, sin_r, -sin_c, sin_c], axis=-1)
    return jnp.asarray(cos_t), jnp.asarray(sin_t)


def kernel(x, c, ctx, c_ctx, w_mod, b_mod, norm1_g, norm2_g, norm3_g, ffn1_w_in, ffn1_w_out,
           w_in, conv_w, q_norm_g, k_norm_g, w_branch_conv, w_branch_attn, w_out, ffn2_w_in,
           ffn2_w_out, final_g):
    bsz, seq, d = x.shape
    depth = w_mod.shape[0]
    assert depth == 1, "single-layer problem"
    d_conv = conv_w.shape[-1]
    q_w = N_Q_HEADS * HEAD_DIM
    kv_w = N_KV_HEADS * HEAD_DIM
    widths = (d_conv, d_conv, d_conv, q_w, kv_w, kv_w)
    splits = tuple(int(sum(widths[:j + 1])) for j in range(len(widths)))
    assert w_in.shape[-1] == splits[-1] + 2 * d

    ctx_row = bsz
    pad_rows = -(bsz + 1) % SUBLANES
    cvec = jnp.concatenate([c, c_ctx[None, :], jnp.zeros((pad_rows, d), F32)], axis=0)
    mod = _mod_call(cvec, w_mod[0], b_mod[0]).reshape(cvec.shape[0], N_MOD, d)

    w1_in = ffn1_w_in[0].astype(BF16)
    w1_out = ffn1_w_out[0].astype(BF16)
    w2_in = ffn2_w_in[0].astype(BF16)
    w2_out = ffn2_w_out[0].astype(BF16)
    w_in_b = w_in[0].astype(BF16)
    w_bc = w_branch_conv[0].astype(BF16)
    w_ba = w_branch_attn[0].astype(BF16)
    w_o = w_out[0].astype(BF16)

    x1 = _ffn_call(x, mod, lambda b: b, norm1_g[0], w1_in, w1_out, mod_base=0, tm=1024, n_sub=4,
                   name="ffn1_latent")
    cx1 = _ffn_call(ctx, mod, lambda b: ctx_row, norm1_g[0], w1_in, w1_out, mod_base=0,
                    tm=ctx.shape[1], n_sub=1, name="ffn1_context")

    cos_t, sin_t = _rope_tables(seq)
    ck = ctx.shape[1]
    u, bg, gt, qt, k_l, vt_l = _proj_call(x1, mod, norm2_g[0], w_in_b, cos_t, sin_t, q_norm_g[0],
                                          k_norm_g[0], splits=splits, tm=1024, ck=ck, n_sub=4)
    k_c, vt_c = _proj_ctx_call(cx1, mod, ctx_row, norm2_g[0], w_in_b, k_norm_g[0], splits=splits)
    o = _attn_call(qt, k_c, k_l, vt_c, vt_l, tq=1024, qb=256, unroll=8, lag=4)

    return _merge_call(x1, u, bg, gt, o, mod, conv_w[0], w_bc, w_ba, w_o, norm3_g[0], w2_in, w2_out,
                       final_g, tm=512, n_sub=2)
```

```python
import functools
from typing import NamedTuple

import jax
import jax.numpy as jnp
import numpy as np
from jax import lax
from jax.experimental import pallas as pl
from jax.experimental.pallas import tpu as pltpu

F32 = jnp.float32
BF16 = jnp.bfloat16

HEAD_DIM = 128
N_Q_HEADS = 8
N_KV_HEADS = 2
GROUP = N_Q_HEADS // N_KV_HEADS
GRID_W = 64
ROPE_THETA = 10000.0
EPS = 1e-6
N_MOD = 9

V7X_VMEM_BYTES = 64 * 1024 * 1024
VMEM_LIMIT = V7X_VMEM_BYTES - 8 * 1024 * 1024
V7X_MXU_COLS = 256
SUBLANES = 8
BF16_SUBLANES = 16

LOG2E = 1.4426950408889634
MAX_JUMP = 40.0


class _Tiles(NamedTuple):
    ffn_rows: int = 1024
    ffn_sub: int = 4
    merge_rows: int = 512
    merge_sub: int = 2
    attn_queries: int = 1024
    attn_qblock: int = V7X_MXU_COLS
    attn_unroll: int = 8
    attn_lag: int = 4


TILES = _Tiles()


def _const_spec(shape):
    nd = len(shape)
    return pl.BlockSpec(shape, lambda *_: (0,) * nd, pipeline_mode=pl.Buffered(1))


def _rms(x, g):
    return x * lax.rsqrt(jnp.mean(x * x, axis=-1, keepdims=True) + EPS) * g


def _silu(a):
    return a * jax.nn.sigmoid(a)


def _dot(a, b):
    return jnp.dot(a, b, preferred_element_type=F32)


def _mod_kernel(c_ref, w_ref, b_ref, o_ref):
    s = _silu(c_ref[...])
    w = w_ref[...]
    s_hi = s.astype(BF16)
    s_lo = (s - s_hi.astype(F32)).astype(BF16)
    w_hi = w.astype(BF16)
    w_lo = (w - w_hi.astype(F32)).astype(BF16)
    o_ref[...] = _dot(s_hi, w_hi) + (_dot(s_lo, w_hi) + _dot(s_hi, w_lo)) + b_ref[...]


def _mod_call(cvec, w_mod, b_mod):
    rows, d = cvec.shape
    n = w_mod.shape[1]
    tn = d
    return pl.pallas_call(
        _mod_kernel,
        grid=(n // tn,),
        in_specs=[
            pl.BlockSpec((rows, d), lambda j: (0, 0)),
            pl.BlockSpec((d, tn), lambda j: (0, j)),
            pl.BlockSpec((1, tn), lambda j: (0, j)),
        ],
        out_specs=pl.BlockSpec((rows, tn), lambda j: (0, j)),
        out_shape=jax.ShapeDtypeStruct((rows, n), F32),
        name="mod",
    )(cvec, w_mod, b_mod.reshape(1, n))


def _row_split(tm, n_sub):
    assert tm % n_sub == 0
    return [slice(r * (tm // n_sub), (r + 1) * (tm // n_sub)) for r in range(n_sub)]


def _ffn_body(xs, shift, scale, gate, g, win_ref, wout_ref, d_ff):
    hbs = [(_rms(x, g) * (1.0 + scale) + shift).astype(BF16) for x in xs]
    acts = []
    for hb in hbs:
        a = _dot(hb, win_ref[:, :d_ff])
        b = _dot(hb, win_ref[:, d_ff:])
        acts.append((_silu(a) * b).astype(BF16))
    return [x + (0.5 * gate) * _dot(act, wout_ref[...]) for x, act in zip(xs, acts)]


def _ffn_kernel(x_ref, mod_ref, g_ref, win_ref, wout_ref, o_ref, *, mod_base, d_ff, n_sub):
    m = mod_ref[0]
    rows = _row_split(x_ref.shape[1], n_sub)
    outs = _ffn_body([x_ref[0, r, :] for r in rows], m[mod_base:mod_base + 1],
                     m[mod_base + 1:mod_base + 2], m[mod_base + 2:mod_base + 3], g_ref[...],
                     win_ref, wout_ref, d_ff)
    for r, out in zip(rows, outs):
        o_ref[0, r, :] = out


def _ffn_call(x, mod, mod_row, g, w_in, w_out, *, mod_base, tm, n_sub, name):
    bsz, seq, d = x.shape
    d_ff = w_out.shape[0]
    kern = functools.partial(_ffn_kernel, mod_base=mod_base, d_ff=d_ff, n_sub=n_sub)
    return pl.pallas_call(
        kern,
        grid=(bsz, seq // tm),
        in_specs=[
            pl.BlockSpec((1, tm, d), lambda b, i: (b, i, 0)),
            pl.BlockSpec((1, N_MOD, d), lambda b, i: (mod_row(b), 0, 0)),
            _const_spec((1, d)),
            _const_spec(w_in.shape),
            _const_spec(w_out.shape),
        ],
        out_specs=pl.BlockSpec((1, tm, d), lambda b, i: (b, i, 0)),
        out_shape=jax.ShapeDtypeStruct(x.shape, F32),
        compiler_params=pltpu.CompilerParams(
            dimension_semantics=("parallel", "parallel"), vmem_limit_bytes=VMEM_LIMIT),
        name=name,
    )(x, mod, g.reshape(1, d), w_in, w_out)


def _rope_rot(x):
    n = x.shape[-1]
    lane = lax.broadcasted_iota(jnp.int32, x.shape, x.ndim - 1)
    quarter = HEAD_DIM // 4
    fwd = pltpu.roll(x, n - quarter, x.ndim - 1)
    bwd = pltpu.roll(x, quarter, x.ndim - 1)
    return jnp.where((lane % (2 * quarter)) < quarter, fwd, bwd)


def _head_norm(xh, g):
    return xh * lax.rsqrt(jnp.mean(xh * xh, axis=-1, keepdims=True) + EPS) * g


def _proj_kernel(x_ref, mod_ref, g_ref, w_ref, cos_ref, sin_ref, qg_ref, kg_ref,
                 u_ref, bg_ref, gt_ref, qt_ref, k_ref, vt_ref, *, splits, n_sub):
    m = mod_ref[0]
    s_b, s_c, s_v, s_q, s_k, s_vv = splits
    n_in = w_ref.shape[1]
    ck = vt_ref.shape[-1]
    qg = qg_ref[...] * ((HEAD_DIM ** -0.5) * LOG2E)
    rows = _row_split(x_ref.shape[1], n_sub)
    hbs = [(_rms(x_ref[0, r, :], g_ref[...]) * (1.0 + m[4:5]) + m[3:4]).astype(BF16) for r in rows]

    for r, hb in zip(rows, hbs):
        cos = cos_ref[r, :]
        sin = sin_ref[r, :]
        q = _dot(hb, w_ref[:, s_v:s_q])
        k = _dot(hb, w_ref[:, s_q:s_k])
        for hd in range(N_Q_HEADS):
            qh = _head_norm(q[:, hd * HEAD_DIM:(hd + 1) * HEAD_DIM], qg)
            qt_ref[0, hd, :, r] = (qh * cos + _rope_rot(qh) * sin).T.astype(BF16)
        for hd in range(N_KV_HEADS):
            sl = slice(hd * HEAD_DIM, (hd + 1) * HEAD_DIM)
            kh = _head_norm(k[:, sl], kg_ref[...])
            k_ref[0, r, sl] = (kh * cos + _rope_rot(kh) * sin).astype(BF16)
        gt_ref[0, r, :] = jax.nn.sigmoid(_dot(hb, w_ref[:, s_vv:n_in])).astype(BF16)
        v = _dot(hb, w_ref[:, s_k:s_vv])
        for hd in range(N_KV_HEADS):
            for c in range((r.stop - r.start) // ck):
                vc = v[c * ck:(c + 1) * ck, hd * HEAD_DIM:(hd + 1) * HEAD_DIM]
                vt_ref[0, hd, r.start // ck + c] = vc.T.astype(BF16)
        u_ref[0, r, :] = (_dot(hb, w_ref[:, s_b:s_c]) * _dot(hb, w_ref[:, s_c:s_v])).astype(BF16)
        bg_ref[0, r, :] = _dot(hb, w_ref[:, :s_b]).astype(BF16)


def _proj_call(x, mod, g, w_in, cos_t, sin_t, q_g, k_g, *, splits, tm, ck, n_sub):
    bsz, seq, d = x.shape
    d_conv = splits[0]
    kv_w = splits[4] - splits[3]
    gates_w = w_in.shape[1] - splits[5]
    assert (tm // n_sub) % ck == 0
    kern = functools.partial(_proj_kernel, splits=splits, n_sub=n_sub)
    tok = lambda w: pl.BlockSpec((1, tm, w), lambda b, i: (b, i, 0))
    return pl.pallas_call(
        kern,
        grid=(bsz, seq // tm),
        in_specs=[
            tok(d),
            pl.BlockSpec((1, N_MOD, d), lambda b, i: (b, 0, 0)),
            _const_spec((1, d)),
            _const_spec(w_in.shape),
            pl.BlockSpec((tm, HEAD_DIM), lambda b, i: (i, 0)),
            pl.BlockSpec((tm, HEAD_DIM), lambda b, i: (i, 0)),
            _const_spec((1, HEAD_DIM)),
            _const_spec((1, HEAD_DIM)),
        ],
        out_specs=[
            tok(d_conv), tok(d_conv), tok(gates_w),
            pl.BlockSpec((1, N_Q_HEADS, HEAD_DIM, tm), lambda b, i: (b, 0, 0, i)),
            tok(kv_w),
            pl.BlockSpec((1, N_KV_HEADS, tm // ck, HEAD_DIM, ck), lambda b, i: (b, 0, i, 0, 0)),
        ],
        out_shape=[
            jax.ShapeDtypeStruct((bsz, seq, d_conv), BF16),
            jax.ShapeDtypeStruct((bsz, seq, d_conv), BF16),
            jax.ShapeDtypeStruct((bsz, seq, gates_w), BF16),
            jax.ShapeDtypeStruct((bsz, N_Q_HEADS, HEAD_DIM, seq), BF16),
            jax.ShapeDtypeStruct((bsz, seq, kv_w), BF16),
            jax.ShapeDtypeStruct((bsz, N_KV_HEADS, seq // ck, HEAD_DIM, ck), BF16),
        ],
        compiler_params=pltpu.CompilerParams(
            dimension_semantics=("parallel", "parallel"), vmem_limit_bytes=VMEM_LIMIT),
        name="proj_latent",
    )(x, mod, g.reshape(1, d), w_in, cos_t, sin_t, q_g.reshape(1, HEAD_DIM), k_g.reshape(1, HEAD_DIM))


def _proj_ctx_kernel(x_ref, mod_ref, g_ref, w_ref, kg_ref, k_ref, vt_ref, *, kv_w):
    x = x_ref[0]
    m = mod_ref[0]
    h = _rms(x, g_ref[...]) * (1.0 + m[4:5]) + m[3:4]
    hb = h.astype(BF16)
    k = _dot(hb, w_ref[:, :kv_w])
    v = _dot(hb, w_ref[:, kv_w:])
    for hd in range(N_KV_HEADS):
        sl = slice(hd * HEAD_DIM, (hd + 1) * HEAD_DIM)
        k_ref[0, :, sl] = _head_norm(k[:, sl], kg_ref[...]).astype(BF16)
        vt_ref[0, hd] = v[:, sl].T.astype(BF16)


def _proj_ctx_call(cx, mod, mod_row, g, w_in, k_g, *, splits):
    bsz, seq, d = cx.shape
    kv_w = splits[4] - splits[3]
    col_blk = splits[3] // (2 * kv_w)
    assert col_blk * 2 * kv_w == splits[3]
    kern = functools.partial(_proj_ctx_kernel, kv_w=kv_w)
    return pl.pallas_call(
        kern,
        grid=(bsz,),
        in_specs=[
            pl.BlockSpec((1, seq, d), lambda b: (b, 0, 0)),
            pl.BlockSpec((1, N_MOD, d), lambda b: (mod_row, 0, 0)),
            _const_spec((1, d)),
            pl.BlockSpec((d, 2 * kv_w), lambda b: (0, col_blk)),
            _const_spec((1, HEAD_DIM)),
        ],
        out_specs=[
            pl.BlockSpec((1, seq, kv_w), lambda b: (b, 0, 0)),
            pl.BlockSpec((1, N_KV_HEADS, HEAD_DIM, seq), lambda b: (b, 0, 0, 0)),
        ],
        out_shape=[
            jax.ShapeDtypeStruct((bsz, seq, kv_w), BF16),
            jax.ShapeDtypeStruct((bsz, N_KV_HEADS, HEAD_DIM, seq), BF16),
        ],
        compiler_params=pltpu.CompilerParams(dimension_semantics=("parallel",)),
        name="proj_context",
    )(cx, mod, g.reshape(1, d), w_in, k_g.reshape(1, HEAD_DIM))


def _attn_kernel(qt_ref, kc_ref, kl_ref, vtc_ref, vtl_ref, o_ref, ref_ref, max_ref, jump_ref, acc_ref,
                 *, qb, unroll, lag):
    d, tq = qt_ref.shape[2:]
    ck = vtl_ref.shape[-1]
    n_lat = vtl_ref.shape[2]
    blocks = [(g, t) for g in range(GROUP) for t in range(tq // qb)]

    def scores(k, b):
        g, t = blocks[b]
        return _dot(k(), qt_ref[0, g, :, t * qb:(t + 1) * qb])

    def accumulate(vt, b, p, alpha):
        vt_ext = jnp.concatenate([vt(), jnp.ones((BF16_SUBLANES, p.shape[0]), BF16)], axis=0)
        acc_ref[b] = alpha * acc_ref[b] + _dot(vt_ext, p)

    def update_exact(vt, b, s):
        r_old = ref_ref[b]
        r = jnp.maximum(r_old, jnp.max(s, axis=0, keepdims=True))
        accumulate(vt, b, jnp.exp2(s - r).astype(BF16), jnp.exp2(r_old - r))
        ref_ref[b] = r
        max_ref[b] = r

    def update_stale(vt, b, s):
        r = max_ref[b]
        new_max = jnp.maximum(r, jnp.max(s, axis=0, keepdims=True))
        accumulate(vt, b, jnp.exp2(s - r).astype(BF16), jnp.exp2(ref_ref[b] - r))
        jump_ref[b] = jnp.maximum(jump_ref[b], new_max - r)
        ref_ref[b] = r
        max_ref[b] = new_max

    def run(chunks, update, lag):
        items = [(k, vt, b) for k, vt in chunks for b in range(len(blocks))]
        pending = {}
        for i in range(len(items) + lag):
            if i < len(items):
                k, _, b = items[i]
                pending[i] = scores(k, b)
            if i >= lag:
                _, vt, b = items[i - lag]
                update(vt, b, pending.pop(i - lag))

    def latent_chunk(j):
        off = pl.multiple_of(j * ck, ck)
        return (lambda: kl_ref[0, pl.ds(off, ck), :]), (lambda: vtl_ref[0, 0, j])

    def attend(update_latent, unroll, lag):
        ref_ref[...] = jnp.full(ref_ref.shape, -jnp.inf, F32)
        acc_ref[...] = jnp.zeros(acc_ref.shape, F32)
        run([((lambda: kc_ref[0]), (lambda: vtc_ref[0, 0]))], update_exact, lag)

        def body(it, carry):
            run([latent_chunk(it * unroll + u) for u in range(unroll)], update_latent, lag)
            return carry

        lax.fori_loop(0, n_lat // unroll, body, 0)

    jump_ref[...] = jnp.zeros(jump_ref.shape, F32)
    attend(update_stale, unroll, lag)

    @pl.when(jnp.max(jump_ref[...]) > MAX_JUMP)
    def _():
        attend(update_exact, 1, 1)

    for b, (g, t) in enumerate(blocks):
        acc = acc_ref[b]
        o = (acc[:d] / acc[d:d + 1]).T
        o_ref[0, t * qb:(t + 1) * qb, g * d:(g + 1) * d] = o.astype(BF16)


def _attn_call(qt, k_c, k_l, vt_c, vt_l, *, tq, qb, unroll, lag):
    bsz, n_heads, d, seq = qt.shape
    n_lat, _, ck = vt_l.shape[2:]
    assert k_c.shape[1] == ck and vt_c.shape[-1] == ck
    assert tq % qb == 0 and n_lat % unroll == 0
    n_blocks = GROUP * (tq // qb)
    kern = functools.partial(_attn_kernel, qb=qb, unroll=unroll, lag=lag)
    stat = pltpu.VMEM((n_blocks, 1, qb), F32)
    return pl.pallas_call(
        kern,
        grid=(bsz, N_KV_HEADS, seq // tq),
        in_specs=[
            pl.BlockSpec((1, GROUP, d, tq), lambda b, h, i: (b, h, 0, i)),
            pl.BlockSpec((1, ck, d), lambda b, h, i: (b, 0, h)),
            pl.BlockSpec((1, seq, d), lambda b, h, i: (b, 0, h)),
            pl.BlockSpec((1, 1, d, ck), lambda b, h, i: (b, h, 0, 0)),
            pl.BlockSpec((1, 1, n_lat, d, ck), lambda b, h, i: (b, h, 0, 0, 0)),
        ],
        out_specs=pl.BlockSpec((1, tq, GROUP * d), lambda b, h, i: (b, i, h)),
        out_shape=jax.ShapeDtypeStruct((bsz, seq, n_heads * d), BF16),
        scratch_shapes=[stat, stat, stat, pltpu.VMEM((n_blocks, d + BF16_SUBLANES, qb), F32)],
        compiler_params=pltpu.CompilerParams(
            dimension_semantics=("parallel", "parallel", "arbitrary"), vmem_limit_bytes=VMEM_LIMIT),
        name="attention",
    )(qt, k_c, k_l, vt_c, vt_l)


def _merge_kernel(x_ref, u_ref, up_ref, un_ref, bg_ref, gt_ref, o_ref, mod_ref, cw_ref,
                  wbc_ref, wba_ref, wout_ref, g3_ref, win2_ref, wout2_ref, fg_ref, out_ref,
                  *, tm, d_ff, n_sub):
    i = pl.program_id(1)
    last = pl.num_programs(1) - 1
    x = x_ref[0]
    m = mod_ref[0]
    u = u_ref[0].astype(F32)
    d = x.shape[-1]

    row = lax.broadcasted_iota(jnp.int32, (tm, 1), 0)
    halo = up_ref.shape[1]
    prev_edge = jnp.where(i > 0, up_ref[0, halo - 1:halo, :].astype(F32), 0.0)
    next_edge = jnp.where(i < last, un_ref[0, 0:1, :].astype(F32), 0.0)
    u_prev = jnp.where(row == 0, prev_edge, pltpu.roll(u, 1, 0))
    u_next = jnp.where(row == tm - 1, next_edge, pltpu.roll(u, tm - 1, 0))
    cw = cw_ref[...]
    y_conv = bg_ref[0].astype(F32) * (u_prev * cw[0:1] + u * cw[1:2] + u_next * cw[2:3])

    rows = _row_split(tm, n_sub)
    yb = y_conv.astype(BF16)
    merged = [(gt_ref[0, r, :d].astype(F32) * _dot(yb[r], wbc_ref[...])
               + gt_ref[0, r, d:].astype(F32) * _dot(o_ref[0, r, :], wba_ref[...])).astype(BF16)
              for r in rows]
    xs = [x[r] + m[5:6] * _dot(mg, wout_ref[...]) for r, mg in zip(rows, merged)]
    xs = _ffn_body(xs, m[6:7], m[7:8], m[8:9], g3_ref[...], win2_ref, wout2_ref, d_ff)
    for r, xr in zip(rows, xs):
        out_ref[0, r, :] = _rms(xr, fg_ref[...])


def _merge_call(x, u, bg, gt, o, mod, conv_w, w_bc, w_ba, w_out, g3, w_in2, w_out2, fg, *, tm, n_sub):
    bsz, seq, d = x.shape
    d_ff = w_out2.shape[0]
    halo = BF16_SUBLANES if u.dtype == BF16 else SUBLANES
    hb = tm // halo
    n_halo = seq // halo
    kern = functools.partial(_merge_kernel, tm=tm, d_ff=d_ff, n_sub=n_sub)
    tok = lambda w: pl.BlockSpec((1, tm, w), lambda b, i: (b, i, 0))
    return pl.pallas_call(
        kern,
        grid=(bsz, seq // tm),
        in_specs=[
            tok(d),
            tok(u.shape[-1]),
            pl.BlockSpec((1, halo, u.shape[-1]), lambda b, i: (b, jnp.maximum(i * hb - 1, 0), 0)),
            pl.BlockSpec((1, halo, u.shape[-1]),
                         lambda b, i: (b, jnp.minimum((i + 1) * hb, n_halo - 1), 0)),
            tok(bg.shape[-1]),
            tok(gt.shape[-1]),
            tok(o.shape[-1]),
            pl.BlockSpec((1, N_MOD, d), lambda b, i: (b, 0, 0)),
            _const_spec(conv_w.shape),
            _const_spec(w_bc.shape),
            _const_spec(w_ba.shape),
            _const_spec(w_out.shape),
            _const_spec((1, d)),
            _const_spec(w_in2.shape),
            _const_spec(w_out2.shape),
            _const_spec((1, d)),
        ],
        out_specs=tok(d),
        out_shape=jax.ShapeDtypeStruct(x.shape, F32),
        compiler_params=pltpu.CompilerParams(
            dimension_semantics=("parallel", "parallel"), vmem_limit_bytes=VMEM_LIMIT),
        name="merge_ffn2",
    )(x, u, u, u, bg, gt, o, mod, conv_w, w_bc, w_ba, w_out, g3.reshape(1, d), w_in2, w_out2,
      fg.reshape(1, d))


def _rope_tables(seq):
    f32 = np.float32
    pos = np.arange(seq)
    n_freq = HEAD_DIM // 4
    inv = np.power(f32(ROPE_THETA), -np.arange(n_freq, dtype=f32) / f32(n_freq)).astype(f32)
    ang_r = (pos // GRID_W).astype(f32)[:, None] * inv
    ang_c = (pos % GRID_W).astype(f32)[:, None] * inv
    cos_r, sin_r, cos_c, sin_c = (fn(a.astype(np.float64)).astype(f32)
                                  for a in (ang_r, ang_c) for fn in (np.cos, np.sin))
    cos_t = np.concatenate([cos_r, cos_r, cos_c, cos_c], axis=-1)
    sin_t = np.concatenate([-sin_r, sin_r, -sin_c, sin_c], axis=-1)
    return jnp.asarray(cos_t), jnp.asarray(sin_t)


def kernel(x, c, ctx, c_ctx, w_mod, b_mod, norm1_g, norm2_g, norm3_g, ffn1_w_in, ffn1_w_out,
           w_in, conv_w, q_norm_g, k_norm_g, w_branch_conv, w_branch_attn, w_out, ffn2_w_in,
           ffn2_w_out, final_g):
    bsz, seq, d = x.shape
    depth = w_mod.shape[0]
    assert depth == 1, "single-layer problem"
    d_conv = conv_w.shape[-1]
    q_w = N_Q_HEADS * HEAD_DIM
    kv_w = N_KV_HEADS * HEAD_DIM
    widths = (d_conv, d_conv, d_conv, q_w, kv_w, kv_w)
    splits = tuple(int(sum(widths[:j + 1])) for j in range(len(widths)))
    assert w_in.shape[-1] == splits[-1] + 2 * d

    ctx_row = bsz
    pad_rows = -(bsz + 1) % SUBLANES
    cvec = jnp.concatenate([c, c_ctx[None, :], jnp.zeros((pad_rows, d), F32)], axis=0)
    mod = _mod_call(cvec, w_mod[0], b_mod[0]).reshape(cvec.shape[0], N_MOD, d)

    w1_in = ffn1_w_in[0].astype(BF16)
    w1_out = ffn1_w_out[0].astype(BF16)
    w2_in = ffn2_w_in[0].astype(BF16)
    w2_out = ffn2_w_out[0].astype(BF16)
    w_in_b = w_in[0].astype(BF16)
    w_bc = w_branch_conv[0].astype(BF16)
    w_ba = w_branch_attn[0].astype(BF16)
    w_o = w_out[0].astype(BF16)

    t = TILES
    x1 = _ffn_call(x, mod, lambda b: b, norm1_g[0], w1_in, w1_out, mod_base=0, tm=t.ffn_rows,
                   n_sub=t.ffn_sub, name="ffn1_latent")
    cx1 = _ffn_call(ctx, mod, lambda b: ctx_row, norm1_g[0], w1_in, w1_out, mod_base=0,
                    tm=ctx.shape[1], n_sub=1, name="ffn1_context")

    cos_t, sin_t = _rope_tables(seq)
    ck = ctx.shape[1]
    u, bg, gt, qt, k_l, vt_l = _proj_call(x1, mod, norm2_g[0], w_in_b, cos_t, sin_t, q_norm_g[0],
                                          k_norm_g[0], splits=splits, tm=t.ffn_rows, ck=ck,
                                          n_sub=t.ffn_sub)
    k_c, vt_c = _proj_ctx_call(cx1, mod, ctx_row, norm2_g[0], w_in_b, k_norm_g[0], splits=splits)
    o = _attn_call(qt, k_c, k_l, vt_c, vt_l, tq=t.attn_queries, qb=t.attn_qblock,
                   unroll=t.attn_unroll, lag=t.attn_lag)

    return _merge_call(x1, u, bg, gt, o, mod, conv_w[0], w_bc, w_ba, w_o, norm3_g[0], w2_in, w2_out,
                       final_g, tm=t.merge_rows, n_sub=t.merge_sub)
```

```python
import functools
from typing import NamedTuple

import jax
import jax.numpy as jnp
import numpy as np
from jax import lax
from jax.experimental import pallas as pl
from jax.experimental.pallas import tpu as pltpu

F32 = jnp.float32
BF16 = jnp.bfloat16

HEAD_DIM = 128
N_Q_HEADS = 8
N_KV_HEADS = 2
GROUP = N_Q_HEADS // N_KV_HEADS
GRID_W = 64
ROPE_THETA = 10000.0
EPS = 1e-6
N_MOD = 9

V7X_VMEM_BYTES = 64 * 1024 * 1024
VMEM_LIMIT = V7X_VMEM_BYTES - 8 * 1024 * 1024
V7X_MXU_COLS = 256
SUBLANES = 8
BF16_SUBLANES = 16

LOG2E = 1.4426950408889634
MAX_JUMP = 40.0


class _Tiles(NamedTuple):
    ffn_rows: int = 1024
    ffn_sub: int = 4
    merge_rows: int = 512
    merge_sub: int = 2
    attn_queries: int = 2048
    attn_qblock: int = V7X_MXU_COLS
    attn_unroll: int = 4
    attn_lag: int = 4


TILES = _Tiles()


def _const_spec(shape):
    nd = len(shape)
    return pl.BlockSpec(shape, lambda *_: (0,) * nd, pipeline_mode=pl.Buffered(1))


def _rms(x, g):
    return x * lax.rsqrt(jnp.mean(x * x, axis=-1, keepdims=True) + EPS) * g


def _silu(a):
    return a * jax.nn.sigmoid(a)


def _dot(a, b):
    return jnp.dot(a, b, preferred_element_type=F32)


def _mod_kernel(c_ref, w_ref, b_ref, o_ref):
    s = _silu(c_ref[...])
    w = w_ref[...]
    s_hi = s.astype(BF16)
    s_lo = (s - s_hi.astype(F32)).astype(BF16)
    w_hi = w.astype(BF16)
    w_lo = (w - w_hi.astype(F32)).astype(BF16)
    o_ref[...] = _dot(s_hi, w_hi) + (_dot(s_lo, w_hi) + _dot(s_hi, w_lo)) + b_ref[...]


def _mod_call(cvec, w_mod, b_mod):
    rows, d = cvec.shape
    n = w_mod.shape[1]
    tn = d
    return pl.pallas_call(
        _mod_kernel,
        grid=(n // tn,),
        in_specs=[
            pl.BlockSpec((rows, d), lambda j: (0, 0)),
            pl.BlockSpec((d, tn), lambda j: (0, j)),
            pl.BlockSpec((1, tn), lambda j: (0, j)),
        ],
        out_specs=pl.BlockSpec((rows, tn), lambda j: (0, j)),
        out_shape=jax.ShapeDtypeStruct((rows, n), F32),
        name="mod",
    )(cvec, w_mod, b_mod.reshape(1, n))


def _row_split(tm, n_sub):
    assert tm % n_sub == 0
    return [slice(r * (tm // n_sub), (r + 1) * (tm // n_sub)) for r in range(n_sub)]


def _ffn_body(xs, shift, scale, gate, g, win_ref, wout_ref, d_ff):
    hbs = [(_rms(x, g) * (1.0 + scale) + shift).astype(BF16) for x in xs]
    acts = []
    for hb in hbs:
        a = _dot(hb, win_ref[:, :d_ff])
        b = _dot(hb, win_ref[:, d_ff:])
        acts.append((_silu(a) * b).astype(BF16))
    return [x + (0.5 * gate) * _dot(act, wout_ref[...]) for x, act in zip(xs, acts)]


def _ffn_kernel(x_ref, mod_ref, g_ref, win_ref, wout_ref, o_ref, *, mod_base, d_ff, n_sub):
    m = mod_ref[0]
    rows = _row_split(x_ref.shape[1], n_sub)
    outs = _ffn_body([x_ref[0, r, :] for r in rows], m[mod_base:mod_base + 1],
                     m[mod_base + 1:mod_base + 2], m[mod_base + 2:mod_base + 3], g_ref[...],
                     win_ref, wout_ref, d_ff)
    for r, out in zip(rows, outs):
        o_ref[0, r, :] = out


def _ffn_call(x, mod, mod_row, g, w_in, w_out, *, mod_base, tm, n_sub, name):
    bsz, seq, d = x.shape
    d_ff = w_out.shape[0]
    kern = functools.partial(_ffn_kernel, mod_base=mod_base, d_ff=d_ff, n_sub=n_sub)
    return pl.pallas_call(
        kern,
        grid=(bsz, seq // tm),
        in_specs=[
            pl.BlockSpec((1, tm, d), lambda b, i: (b, i, 0)),
            pl.BlockSpec((1, N_MOD, d), lambda b, i: (mod_row(b), 0, 0)),
            _const_spec((1, d)),
            _const_spec(w_in.shape),
            _const_spec(w_out.shape),
        ],
        out_specs=pl.BlockSpec((1, tm, d), lambda b, i: (b, i, 0)),
        out_shape=jax.ShapeDtypeStruct(x.shape, F32),
        compiler_params=pltpu.CompilerParams(
            dimension_semantics=("parallel", "parallel"), vmem_limit_bytes=VMEM_LIMIT),
        name=name,
    )(x, mod, g.reshape(1, d), w_in, w_out)


def _rope_rot(x):
    n = x.shape[-1]
    lane = lax.broadcasted_iota(jnp.int32, x.shape, x.ndim - 1)
    quarter = HEAD_DIM // 4
    fwd = pltpu.roll(x, n - quarter, x.ndim - 1)
    bwd = pltpu.roll(x, quarter, x.ndim - 1)
    return jnp.where((lane % (2 * quarter)) < quarter, fwd, bwd)


def _head_norm(xh, g):
    return xh * lax.rsqrt(jnp.mean(xh * xh, axis=-1, keepdims=True) + EPS) * g


def _proj_kernel(x_ref, mod_ref, g_ref, w_ref, cos_ref, sin_ref, qg_ref, kg_ref,
                 u_ref, bg_ref, gt_ref, qt_ref, k_ref, vt_ref, *, splits, n_sub):
    m = mod_ref[0]
    s_b, s_c, s_v, s_q, s_k, s_vv = splits
    n_in = w_ref.shape[1]
    ck = vt_ref.shape[-1]
    qg = qg_ref[...] * ((HEAD_DIM ** -0.5) * LOG2E)
    rows = _row_split(x_ref.shape[1], n_sub)
    hbs = [(_rms(x_ref[0, r, :], g_ref[...]) * (1.0 + m[4:5]) + m[3:4]).astype(BF16) for r in rows]

    for r, hb in zip(rows, hbs):
        cos = cos_ref[r, :]
        sin = sin_ref[r, :]
        q = _dot(hb, w_ref[:, s_v:s_q])
        k = _dot(hb, w_ref[:, s_q:s_k])
        for hd in range(N_Q_HEADS):
            qh = _head_norm(q[:, hd * HEAD_DIM:(hd + 1) * HEAD_DIM], qg)
            qt_ref[0, hd, :, r] = (qh * cos + _rope_rot(qh) * sin).T.astype(BF16)
        for hd in range(N_KV_HEADS):
            sl = slice(hd * HEAD_DIM, (hd + 1) * HEAD_DIM)
            kh = _head_norm(k[:, sl], kg_ref[...])
            k_ref[0, r, sl] = (kh * cos + _rope_rot(kh) * sin).astype(BF16)
        gt_ref[0, r, :] = jax.nn.sigmoid(_dot(hb, w_ref[:, s_vv:n_in])).astype(BF16)
        v = _dot(hb, w_ref[:, s_k:s_vv])
        for hd in range(N_KV_HEADS):
            for c in range((r.stop - r.start) // ck):
                vc = v[c * ck:(c + 1) * ck, hd * HEAD_DIM:(hd + 1) * HEAD_DIM]
                vt_ref[0, hd, r.start // ck + c] = vc.T.astype(BF16)
        u_ref[0, r, :] = (_dot(hb, w_ref[:, s_b:s_c]) * _dot(hb, w_ref[:, s_c:s_v])).astype(BF16)
        bg_ref[0, r, :] = _dot(hb, w_ref[:, :s_b]).astype(BF16)


def _proj_call(x, mod, g, w_in, cos_t, sin_t, q_g, k_g, *, splits, tm, ck, n_sub):
    bsz, seq, d = x.shape
    d_conv = splits[0]
    kv_w = splits[4] - splits[3]
    gates_w = w_in.shape[1] - splits[5]
    assert (tm // n_sub) % ck == 0
    kern = functools.partial(_proj_kernel, splits=splits, n_sub=n_sub)
    tok = lambda w: pl.BlockSpec((1, tm, w), lambda b, i: (b, i, 0))
    return pl.pallas_call(
        kern,
        grid=(bsz, seq // tm),
        in_specs=[
            tok(d),
            pl.BlockSpec((1, N_MOD, d), lambda b, i: (b, 0, 0)),
            _const_spec((1, d)),
            _const_spec(w_in.shape),
            pl.BlockSpec((tm, HEAD_DIM), lambda b, i: (i, 0)),
            pl.BlockSpec((tm, HEAD_DIM), lambda b, i: (i, 0)),
            _const_spec((1, HEAD_DIM)),
            _const_spec((1, HEAD_DIM)),
        ],
        out_specs=[
            tok(d_conv), tok(d_conv), tok(gates_w),
            pl.BlockSpec((1, N_Q_HEADS, HEAD_DIM, tm), lambda b, i: (b, 0, 0, i)),
            tok(kv_w),
            pl.BlockSpec((1, N_KV_HEADS, tm // ck, HEAD_DIM, ck), lambda b, i: (b, 0, i, 0, 0)),
        ],
        out_shape=[
            jax.ShapeDtypeStruct((bsz, seq, d_conv), BF16),
            jax.ShapeDtypeStruct((bsz, seq, d_conv), BF16),
            jax.ShapeDtypeStruct((bsz, seq, gates_w), BF16),
            jax.ShapeDtypeStruct((bsz, N_Q_HEADS, HEAD_DIM, seq), BF16),
            jax.ShapeDtypeStruct((bsz, seq, kv_w), BF16),
            jax.ShapeDtypeStruct((bsz, N_KV_HEADS, seq // ck, HEAD_DIM, ck), BF16),
        ],
        compiler_params=pltpu.CompilerParams(
            dimension_semantics=("parallel", "parallel"), vmem_limit_bytes=VMEM_LIMIT),
        name="proj_latent",
    )(x, mod, g.reshape(1, d), w_in, cos_t, sin_t, q_g.reshape(1, HEAD_DIM), k_g.reshape(1, HEAD_DIM))


def _proj_ctx_kernel(x_ref, mod_ref, g_ref, w_ref, kg_ref, k_ref, vt_ref, *, kv_w):
    x = x_ref[0]
    m = mod_ref[0]
    h = _rms(x, g_ref[...]) * (1.0 + m[4:5]) + m[3:4]
    hb = h.astype(BF16)
    k = _dot(hb, w_ref[:, :kv_w])
    v = _dot(hb, w_ref[:, kv_w:])
    for hd in range(N_KV_HEADS):
        sl = slice(hd * HEAD_DIM, (hd + 1) * HEAD_DIM)
        k_ref[0, :, sl] = _head_norm(k[:, sl], kg_ref[...]).astype(BF16)
        vt_ref[0, hd] = v[:, sl].T.astype(BF16)


def _proj_ctx_call(cx, mod, mod_row, g, w_in, k_g, *, splits):
    bsz, seq, d = cx.shape
    kv_w = splits[4] - splits[3]
    col_blk = splits[3] // (2 * kv_w)
    assert col_blk * 2 * kv_w == splits[3]
    kern = functools.partial(_proj_ctx_kernel, kv_w=kv_w)
    return pl.pallas_call(
        kern,
        grid=(bsz,),
        in_specs=[
            pl.BlockSpec((1, seq, d), lambda b: (b, 0, 0)),
            pl.BlockSpec((1, N_MOD, d), lambda b: (mod_row, 0, 0)),
            _const_spec((1, d)),
            pl.BlockSpec((d, 2 * kv_w), lambda b: (0, col_blk)),
            _const_spec((1, HEAD_DIM)),
        ],
        out_specs=[
            pl.BlockSpec((1, seq, kv_w), lambda b: (b, 0, 0)),
            pl.BlockSpec((1, N_KV_HEADS, HEAD_DIM, seq), lambda b: (b, 0, 0, 0)),
        ],
        out_shape=[
            jax.ShapeDtypeStruct((bsz, seq, kv_w), BF16),
            jax.ShapeDtypeStruct((bsz, N_KV_HEADS, HEAD_DIM, seq), BF16),
        ],
        compiler_params=pltpu.CompilerParams(dimension_semantics=("parallel",)),
        name="proj_context",
    )(cx, mod, g.reshape(1, d), w_in, k_g.reshape(1, HEAD_DIM))


def _attn_kernel(qt_ref, kc_ref, kl_ref, vtc_ref, vtl_ref, o_ref, ref_ref, max_ref, jump_ref, acc_ref,
                 *, qb, unroll, lag):
    d, tq = qt_ref.shape[2:]
    ck = vtl_ref.shape[-1]
    n_lat = vtl_ref.shape[2]
    blocks = [(g, t) for g in range(GROUP) for t in range(tq // qb)]

    def scores(k, b):
        g, t = blocks[b]
        return _dot(k(), qt_ref[0, g, :, t * qb:(t + 1) * qb])

    def accumulate(vt, b, p, alpha):
        vt_ext = jnp.concatenate([vt(), jnp.ones((BF16_SUBLANES, p.shape[0]), BF16)], axis=0)
        acc_ref[b] = alpha * acc_ref[b] + _dot(vt_ext, p)

    def update_exact(vt, b, s):
        r_old = ref_ref[b]
        r = jnp.maximum(r_old, jnp.max(s, axis=0, keepdims=True))
        accumulate(vt, b, jnp.exp2(s - r).astype(BF16), jnp.exp2(r_old - r))
        ref_ref[b] = r
        max_ref[b] = r

    def update_stale(vt, b, s):
        r = max_ref[b]
        new_max = jnp.maximum(r, jnp.max(s, axis=0, keepdims=True))
        accumulate(vt, b, jnp.exp2(s - r).astype(BF16), jnp.exp2(ref_ref[b] - r))
        jump_ref[b] = jnp.maximum(jump_ref[b], new_max - r)
        ref_ref[b] = r
        max_ref[b] = new_max

    def run(chunks, update, lag):
        items = [(k, vt, b) for k, vt in chunks for b in range(len(blocks))]
        pending = {}
        for i in range(len(items) + lag):
            if i < len(items):
                k, _, b = items[i]
                pending[i] = scores(k, b)
            if i >= lag:
                _, vt, b = items[i - lag]
                update(vt, b, pending.pop(i - lag))

    def latent_chunk(j):
        off = pl.multiple_of(j * ck, ck)
        return (lambda: kl_ref[0, pl.ds(off, ck), :]), (lambda: vtl_ref[0, 0, j])

    def attend(update_latent, unroll, lag):
        ref_ref[...] = jnp.full(ref_ref.shape, -jnp.inf, F32)
        acc_ref[...] = jnp.zeros(acc_ref.shape, F32)
        run([((lambda: kc_ref[0]), (lambda: vtc_ref[0, 0]))], update_exact, lag)

        def body(it, carry):
            run([latent_chunk(it * unroll + u) for u in range(unroll)], update_latent, lag)
            return carry

        lax.fori_loop(0, n_lat // unroll, body, 0)

    jump_ref[...] = jnp.zeros(jump_ref.shape, F32)
    attend(update_stale, unroll, lag)

    @pl.when(jnp.max(jump_ref[...]) > MAX_JUMP)
    def _():
        attend(update_exact, 1, 1)

    for b, (g, t) in enumerate(blocks):
        acc = acc_ref[b]
        o = (acc[:d] / acc[d:d + 1]).T
        o_ref[0, t * qb:(t + 1) * qb, g * d:(g + 1) * d] = o.astype(BF16)


def _attn_call(qt, k_c, k_l, vt_c, vt_l, *, tq, qb, unroll, lag):
    bsz, n_heads, d, seq = qt.shape
    n_lat, _, ck = vt_l.shape[2:]
    assert k_c.shape[1] == ck and vt_c.shape[-1] == ck
    assert tq % qb == 0 and n_lat % unroll == 0
    n_blocks = GROUP * (tq // qb)
    kern = functools.partial(_attn_kernel, qb=qb, unroll=unroll, lag=lag)
    stat = pltpu.VMEM((n_blocks, 1, qb), F32)
    return pl.pallas_call(
        kern,
        grid=(bsz, N_KV_HEADS, seq // tq),
        in_specs=[
            pl.BlockSpec((1, GROUP, d, tq), lambda b, h, i: (b, h, 0, i)),
            pl.BlockSpec((1, ck, d), lambda b, h, i: (b, 0, h)),
            pl.BlockSpec((1, seq, d), lambda b, h, i: (b, 0, h)),
            pl.BlockSpec((1, 1, d, ck), lambda b, h, i: (b, h, 0, 0)),
            pl.BlockSpec((1, 1, n_lat, d, ck), lambda b, h, i: (b, h, 0, 0, 0)),
        ],
        out_specs=pl.BlockSpec((1, tq, GROUP * d), lambda b, h, i: (b, i, h)),
        out_shape=jax.ShapeDtypeStruct((bsz, seq, n_heads * d), BF16),
        scratch_shapes=[stat, stat, stat, pltpu.VMEM((n_blocks, d + BF16_SUBLANES, qb), F32)],
        compiler_params=pltpu.CompilerParams(
            dimension_semantics=("parallel", "parallel", "arbitrary"), vmem_limit_bytes=VMEM_LIMIT),
        name="attention",
    )(qt, k_c, k_l, vt_c, vt_l)


def _merge_kernel(x_ref, u_ref, up_ref, un_ref, bg_ref, gt_ref, o_ref, mod_ref, cw_ref,
                  wbc_ref, wba_ref, wout_ref, g3_ref, win2_ref, wout2_ref, fg_ref, out_ref,
                  *, tm, d_ff, n_sub):
    i = pl.program_id(1)
    last = pl.num_programs(1) - 1
    x = x_ref[0]
    m = mod_ref[0]
    u = u_ref[0].astype(F32)
    d = x.shape[-1]

    row = lax.broadcasted_iota(jnp.int32, (tm, 1), 0)
    halo = up_ref.shape[1]
    prev_edge = jnp.where(i > 0, up_ref[0, halo - 1:halo, :].astype(F32), 0.0)
    next_edge = jnp.where(i < last, un_ref[0, 0:1, :].astype(F32), 0.0)
    u_prev = jnp.where(row == 0, prev_edge, pltpu.roll(u, 1, 0))
    u_next = jnp.where(row == tm - 1, next_edge, pltpu.roll(u, tm - 1, 0))
    cw = cw_ref[...]
    y_conv = bg_ref[0].astype(F32) * (u_prev * cw[0:1] + u * cw[1:2] + u_next * cw[2:3])

    rows = _row_split(tm, n_sub)
    yb = y_conv.astype(BF16)
    merged = [(gt_ref[0, r, :d].astype(F32) * _dot(yb[r], wbc_ref[...])
               + gt_ref[0, r, d:].astype(F32) * _dot(o_ref[0, r, :], wba_ref[...])).astype(BF16)
              for r in rows]
    xs = [x[r] + m[5:6] * _dot(mg, wout_ref[...]) for r, mg in zip(rows, merged)]
    xs = _ffn_body(xs, m[6:7], m[7:8], m[8:9], g3_ref[...], win2_ref, wout2_ref, d_ff)
    for r, xr in zip(rows, xs):
        out_ref[0, r, :] = _rms(xr, fg_ref[...])


def _merge_call(x, u, bg, gt, o, mod, conv_w, w_bc, w_ba, w_out, g3, w_in2, w_out2, fg, *, tm, n_sub):
    bsz, seq, d = x.shape
    d_ff = w_out2.shape[0]
    halo = BF16_SUBLANES if u.dtype == BF16 else SUBLANES
    hb = tm // halo
    n_halo = seq // halo
    kern = functools.partial(_merge_kernel, tm=tm, d_ff=d_ff, n_sub=n_sub)
    tok = lambda w: pl.BlockSpec((1, tm, w), lambda b, i: (b, i, 0))
    return pl.pallas_call(
        kern,
        grid=(bsz, seq // tm),
        in_specs=[
            tok(d),
            tok(u.shape[-1]),
            pl.BlockSpec((1, halo, u.shape[-1]), lambda b, i: (b, jnp.maximum(i * hb - 1, 0), 0)),
            pl.BlockSpec((1, halo, u.shape[-1]),
                         lambda b, i: (b, jnp.minimum((i + 1) * hb, n_halo - 1), 0)),
            tok(bg.shape[-1]),
            tok(gt.shape[-1]),
            tok(o.shape[-1]),
            pl.BlockSpec((1, N_MOD, d), lambda b, i: (b, 0, 0)),
            _const_spec(conv_w.shape),
            _const_spec(w_bc.shape),
            _const_spec(w_ba.shape),
            _const_spec(w_out.shape),
            _const_spec((1, d)),
            _const_spec(w_in2.shape),
            _const_spec(w_out2.shape),
            _const_spec((1, d)),
        ],
        out_specs=tok(d),
        out_shape=jax.ShapeDtypeStruct(x.shape, F32),
        compiler_params=pltpu.CompilerParams(
            dimension_semantics=("parallel", "parallel"), vmem_limit_bytes=VMEM_LIMIT),
        name="merge_ffn2",
    )(x, u, u, u, bg, gt, o, mod, conv_w, w_bc, w_ba, w_out, g3.reshape(1, d), w_in2, w_out2,
      fg.reshape(1, d))


def _rope_tables(seq):
    f32 = np.float32
    pos = np.arange(seq)
    n_freq = HEAD_DIM // 4
    inv = np.power(f32(ROPE_THETA), -np.arange(n_freq, dtype=f32) / f32(n_freq)).astype(f32)
    ang_r = (pos // GRID_W).astype(f32)[:, None] * inv
    ang_c = (pos % GRID_W).astype(f32)[:, None] * inv
    cos_r, sin_r, cos_c, sin_c = (fn(a.astype(np.float64)).astype(f32)
                                  for a in (ang_r, ang_c) for fn in (np.cos, np.sin))
    cos_t = np.concatenate([cos_r, cos_r, cos_c, cos_c], axis=-1)
    sin_t = np.concatenate([-sin_r, sin_r, -sin_c, sin_c], axis=-1)
    return jnp.asarray(cos_t), jnp.asarray(sin_t)


def kernel(x, c, ctx, c_ctx, w_mod, b_mod, norm1_g, norm2_g, norm3_g, ffn1_w_in, ffn1_w_out,
           w_in, conv_w, q_norm_g, k_norm_g, w_branch_conv, w_branch_attn, w_out, ffn2_w_in,
           ffn2_w_out, final_g):
    bsz, seq, d = x.shape
    depth = w_mod.shape[0]
    assert depth == 1, "single-layer problem"
    d_conv = conv_w.shape[-1]
    q_w = N_Q_HEADS * HEAD_DIM
    kv_w = N_KV_HEADS * HEAD_DIM
    widths = (d_conv, d_conv, d_conv, q_w, kv_w, kv_w)
    splits = tuple(int(sum(widths[:j + 1])) for j in range(len(widths)))
    assert w_in.shape[-1] == splits[-1] + 2 * d

    ctx_row = bsz
    pad_rows = -(bsz + 1) % SUBLANES
    cvec = jnp.concatenate([c, c_ctx[None, :], jnp.zeros((pad_rows, d), F32)], axis=0)
    mod = _mod_call(cvec, w_mod[0], b_mod[0]).reshape(cvec.shape[0], N_MOD, d)

    w1_in = ffn1_w_in[0].astype(BF16)
    w1_out = ffn1_w_out[0].astype(BF16)
    w2_in = ffn2_w_in[0].astype(BF16)
    w2_out = ffn2_w_out[0].astype(BF16)
    w_in_b = w_in[0].astype(BF16)
    w_bc = w_branch_conv[0].astype(BF16)
    w_ba = w_branch_attn[0].astype(BF16)
    w_o = w_out[0].astype(BF16)

    t = TILES
    x1 = _ffn_call(x, mod, lambda b: b, norm1_g[0], w1_in, w1_out, mod_base=0, tm=t.ffn_rows,
                   n_sub=t.ffn_sub, name="ffn1_latent")
    cx1 = _ffn_call(ctx, mod, lambda b: ctx_row, norm1_g[0], w1_in, w1_out, mod_base=0,
                    tm=ctx.shape[1], n_sub=1, name="ffn1_context")

    cos_t, sin_t = _rope_tables(seq)
    ck = ctx.shape[1]
    u, bg, gt, qt, k_l, vt_l = _proj_call(x1, mod, norm2_g[0], w_in_b, cos_t, sin_t, q_norm_g[0],
                                          k_norm_g[0], splits=splits, tm=t.ffn_rows, ck=ck,
                                          n_sub=t.ffn_sub)
    k_c, vt_c = _proj_ctx_call(cx1, mod, ctx_row, norm2_g[0], w_in_b, k_norm_g[0], splits=splits)
    o = _attn_call(qt, k_c, k_l, vt_c, vt_l, tq=t.attn_queries, qb=t.attn_qblock,
                   unroll=t.attn_unroll, lag=t.attn_lag)

    return _merge_call(x1, u, bg, gt, o, mod, conv_w[0], w_bc, w_ba, w_o, norm3_g[0], w2_in, w2_out,
                       final_g, tm=t.merge_rows, n_sub=t.merge_sub)
```
